```python
import math
import jax, jax.numpy as jnp
from jax import lax
import numpy as np

D_MODEL = 1024
BATCH = 8
SEQ = 4096
DEPTH = 2

SSD_EXPAND = 2
SSD_D_INNER = SSD_EXPAND * D_MODEL
SSD_HEAD_DIM = 64
SSD_N_HEADS = SSD_D_INNER // SSD_HEAD_DIM
SSD_N_GROUPS = 8
SSD_HEADS_PER_GROUP = SSD_N_HEADS // SSD_N_GROUPS
SSD_D_STATE = 128
SSD_CONV_WIDTH = 4
SSD_CHUNK = 128
SSD_CONV_DIM = SSD_D_INNER + 2 * SSD_N_GROUPS * SSD_D_STATE
SSD_IN_DIM = SSD_D_INNER + SSD_CONV_DIM + SSD_N_HEADS

ATTN_HEAD_DIM = 64
ATTN_N_Q_HEADS = D_MODEL // ATTN_HEAD_DIM
ATTN_N_KV_HEADS = 4
ATTN_REP = ATTN_N_Q_HEADS // ATTN_N_KV_HEADS
ATTN_WINDOW = 128
ATTN_QKV_DIM = (ATTN_N_Q_HEADS + 2 * ATTN_N_KV_HEADS) * ATTN_HEAD_DIM

D_FF = 4 * D_MODEL

N_MIXERS = 2
N_SSD_LAYERS = (DEPTH + 1) // 2
N_ATTN_LAYERS = DEPTH // 2
NORM_EPS = 1e-6

kernel_name = "hybrid_ssd_swa_sink_sqrelu_trunk"


def rms_norm(x, w):
    xf = x.astype(jnp.float32)
    y = xf * lax.rsqrt(jnp.mean(xf * xf, axis=-1, keepdims=True) + NORM_EPS)
    return (y * w.astype(jnp.float32)).astype(x.dtype)


def causal_depthwise_conv(x, w, b):
    c = x.shape[-1]
    y = lax.conv_general_dilated(
        x, w.astype(x.dtype)[:, None, :], window_strides=(1,),
        padding=[(SSD_CONV_WIDTH - 1, 0)],
        dimension_numbers=("NWC", "WIO", "NWC"), feature_group_count=c)
    return y + b.astype(x.dtype)


def ssd_chunked_scan(xs, dt, a, bmat, cmat):
    b, l, g, j, p = xs.shape
    n = bmat.shape[-1]
    nc = l // SSD_CHUNK

    def to_chunks(t):
        t = t.astype(jnp.float32).reshape((b, nc, SSD_CHUNK) + t.shape[2:])
        return jnp.moveaxis(t, 1, 0)

    xc, dtc, bc, cc = to_chunks(xs), to_chunks(dt), to_chunks(bmat), to_chunks(cmat)
    ac = dtc * a.astype(jnp.float32)
    causal = jnp.tril(jnp.ones((SSD_CHUNK, SSD_CHUNK), dtype=bool))[None, :, :, None, None]

    def step(state, inp):
        x_q, dt_q, a_q, b_q, c_q = inp
        cum = jnp.cumsum(a_q, axis=1)
        diff = cum[:, :, None] - cum[:, None, :]
        decay = jnp.exp(jnp.where(causal, diff, -jnp.inf))
        cb = jnp.einsum("btgn,bsgn->btsg", c_q, b_q)
        y_intra = jnp.einsum("btsg,btsgj,bsgj,bsgjp->btgjp", cb, decay, dt_q, x_q)
        y_inter = jnp.einsum("btgn,bgjpn->btgjp", c_q, state) * jnp.exp(cum)[..., None]
        decay_to_end = jnp.exp(cum[:, -1:] - cum)
        new_state = state * jnp.exp(cum[:, -1])[..., None, None] + jnp.einsum(
            "bsgn,bsgj,bsgjp->bgjpn", b_q, dt_q * decay_to_end, x_q)
        return new_state, y_intra + y_inter

    state0 = jnp.zeros((b, g, j, p, n), jnp.float32)
    _, ys = lax.scan(step, state0, (xc, dtc, ac, bc, cc))
    return jnp.moveaxis(ys, 0, 1).reshape(b, l, g, j, p)


def ssd_mixer(u, w_in, conv_w, conv_b, dt_bias, a_log, d_skip, norm_w, w_out):
    b, l, _ = u.shape
    zxbcdt = u @ w_in
    z = zxbcdt[..., :SSD_D_INNER]
    xbc = zxbcdt[..., SSD_D_INNER:SSD_D_INNER + SSD_CONV_DIM]
    dt_raw = zxbcdt[..., SSD_D_INNER + SSD_CONV_DIM:]
    xbc = jax.nn.silu(causal_depthwise_conv(xbc, conv_w, conv_b))
    gn = SSD_N_GROUPS * SSD_D_STATE
    xs = xbc[..., :SSD_D_INNER].reshape(b, l, SSD_N_GROUPS, SSD_HEADS_PER_GROUP, SSD_HEAD_DIM)
    bmat = xbc[..., SSD_D_INNER:SSD_D_INNER + gn].reshape(b, l, SSD_N_GROUPS, SSD_D_STATE)
    cmat = xbc[..., SSD_D_INNER + gn:].reshape(b, l, SSD_N_GROUPS, SSD_D_STATE)
    dt = jax.nn.softplus(dt_raw.astype(jnp.float32) + dt_bias.astype(jnp.float32))
    dt = dt.reshape(b, l, SSD_N_GROUPS, SSD_HEADS_PER_GROUP)
    a = -jnp.exp(a_log.astype(jnp.float32)).reshape(SSD_N_GROUPS, SSD_HEADS_PER_GROUP)
    y = ssd_chunked_scan(xs, dt, a, bmat, cmat)
    y = y + d_skip.astype(jnp.float32).reshape(SSD_N_GROUPS, SSD_HEADS_PER_GROUP, 1) * xs.astype(jnp.float32)
    y = y.reshape(b, l, SSD_D_INNER) * jax.nn.silu(z.astype(jnp.float32))
    y = y.reshape(b, l, SSD_N_GROUPS, SSD_D_INNER // SSD_N_GROUPS)
    y = y * lax.rsqrt(jnp.mean(y * y, axis=-1, keepdims=True) + NORM_EPS)
    y = (y.reshape(b, l, SSD_D_INNER) * norm_w.astype(jnp.float32)).astype(u.dtype)
    return y @ w_out


def swa_sink_attention(u, w_qkv, b_qkv, sinks, w_o, b_o):
    b, l, _ = u.shape
    nb = l // ATTN_WINDOW
    qkv = u @ w_qkv + b_qkv
    qd = ATTN_N_Q_HEADS * ATTN_HEAD_DIM
    kd = ATTN_N_KV_HEADS * ATTN_HEAD_DIM
    q = qkv[..., :qd].reshape(b, nb, ATTN_WINDOW, ATTN_N_KV_HEADS, ATTN_REP, ATTN_HEAD_DIM)
    k = qkv[..., qd:qd + kd].reshape(b, nb, ATTN_WINDOW, ATTN_N_KV_HEADS, ATTN_HEAD_DIM)
    v = qkv[..., qd + kd:].reshape(b, nb, ATTN_WINDOW, ATTN_N_KV_HEADS, ATTN_HEAD_DIM)

    def with_prev_block(t):
        prev = jnp.concatenate([jnp.zeros_like(t[:, :1]), t[:, :-1]], axis=1)
        return jnp.concatenate([prev, t], axis=2)

    kb, vb = with_prev_block(k), with_prev_block(v)
    scores = jnp.einsum("bnqkrd,bnskd->bnkrqs", q, kb).astype(jnp.float32) * (ATTN_HEAD_DIM ** -0.5)
    qpos = jnp.arange(ATTN_WINDOW) + ATTN_WINDOW
    kpos = jnp.arange(2 * ATTN_WINDOW)
    rel = qpos[:, None] - kpos[None, :]
    band = (rel >= 0) & (rel < ATTN_WINDOW)
    blk = jnp.arange(nb)[:, None, None]
    valid = band[None] & ~((blk == 0) & (kpos[None, None, :] < ATTN_WINDOW))
    scores = jnp.where(valid[None, :, None, None], scores, -jnp.inf)
    sink = sinks.astype(jnp.float32).reshape(1, 1, ATTN_N_KV_HEADS, ATTN_REP, 1, 1)
    m = jnp.maximum(jnp.max(scores, axis=-1, keepdims=True), sink)
    e = jnp.exp(scores - m)
    probs = e / (jnp.sum(e, axis=-1, keepdims=True) + jnp.exp(sink - m))
    out = jnp.einsum("bnkrqs,bnskd->bnqkrd", probs.astype(vb.dtype), vb).reshape(b, l, qd)
    return out @ w_o + b_o


def sqrelu_mlp(u, w_up, w_down):
    return jnp.square(jax.nn.relu(u @ w_up)) @ w_down


def setup_inputs(seed: int = 0) -> dict:
    key = jax.random.key(seed)
    ks = jax.random.split(key, 24)
    f32 = jnp.float32

    def normal(k, shape, scale):
        return jax.random.normal(k, shape, f32) * scale

    def gain(k, shape):
        return 1.0 + 0.05 * jax.random.normal(k, shape, f32)

    x = jax.random.normal(ks[0], (BATCH, SEQ, D_MODEL), f32)
    ssd_w_in = normal(ks[1], (N_SSD_LAYERS, D_MODEL, SSD_IN_DIM), D_MODEL ** -0.5)
    ssd_conv_w = normal(ks[2], (N_SSD_LAYERS, SSD_CONV_WIDTH, SSD_CONV_DIM), SSD_CONV_WIDTH ** -0.5)
    ssd_conv_b = normal(ks[3], (N_SSD_LAYERS, SSD_CONV_DIM), 0.02)
    dt0 = jnp.exp(jax.random.uniform(ks[4], (N_SSD_LAYERS, SSD_N_HEADS), f32,
                                     math.log(1e-3), math.log(1e-1)))
    ssd_dt_bias = dt0 + jnp.log(-jnp.expm1(-dt0))
    ssd_a_log = jnp.log(jax.random.uniform(ks[5], (N_SSD_LAYERS, SSD_N_HEADS), f32, 1.0, 16.0))
    ssd_d = gain(ks[6], (N_SSD_LAYERS, SSD_N_HEADS))
    ssd_norm_w = gain(ks[7], (N_SSD_LAYERS, SSD_D_INNER))
    ssd_w_out = normal(ks[8], (N_SSD_LAYERS, SSD_D_INNER, D_MODEL), SSD_D_INNER ** -0.5)
    attn_w_qkv = normal(ks[9], (N_ATTN_LAYERS, D_MODEL, ATTN_QKV_DIM), D_MODEL ** -0.5)
    attn_b_qkv = normal(ks[10], (N_ATTN_LAYERS, ATTN_QKV_DIM), 0.02)
    attn_sinks = normal(ks[11], (N_ATTN_LAYERS, ATTN_N_Q_HEADS), 1.0)
    attn_w_o = normal(ks[12], (N_ATTN_LAYERS, ATTN_N_Q_HEADS * ATTN_HEAD_DIM, D_MODEL),
                      (ATTN_N_Q_HEADS * ATTN_HEAD_DIM) ** -0.5)
    attn_b_o = normal(ks[13], (N_ATTN_LAYERS, D_MODEL), 0.02)
    mlp_w_up = normal(ks[14], (DEPTH, D_MODEL, D_FF), D_MODEL ** -0.5)
    mlp_w_down = normal(ks[15], (DEPTH, D_FF, D_MODEL), D_FF ** -0.5)
    mix_pre_norm = gain(ks[16], (DEPTH, D_MODEL))
    mix_post_norm = gain(ks[17], (DEPTH, D_MODEL))
    ffn_pre_norm = gain(ks[18], (DEPTH, D_MODEL))
    ffn_post_norm = gain(ks[19], (DEPTH, D_MODEL))
    return {
        "x": x,
        "ssd_w_in": ssd_w_in, "ssd_conv_w": ssd_conv_w, "ssd_conv_b": ssd_conv_b,
        "ssd_dt_bias": ssd_dt_bias, "ssd_a_log": ssd_a_log, "ssd_d": ssd_d,
        "ssd_norm_w": ssd_norm_w, "ssd_w_out": ssd_w_out,
        "attn_w_qkv": attn_w_qkv, "attn_b_qkv": attn_b_qkv, "attn_sinks": attn_sinks,
        "attn_w_o": attn_w_o, "attn_b_o": attn_b_o,
        "mlp_w_up": mlp_w_up, "mlp_w_down": mlp_w_down,
        "mix_pre_norm": mix_pre_norm, "mix_post_norm": mix_post_norm,
        "ffn_pre_norm": ffn_pre_norm, "ffn_post_norm": ffn_post_norm,
    }


def reference(x, ssd_w_in, ssd_conv_w, ssd_conv_b, ssd_dt_bias, ssd_a_log, ssd_d,
              ssd_norm_w, ssd_w_out, attn_w_qkv, attn_b_qkv, attn_sinks, attn_w_o,
              attn_b_o, mlp_w_up, mlp_w_down, mix_pre_norm, mix_post_norm,
              ffn_pre_norm, ffn_post_norm):
    h = x
    for i in range(DEPTH):
        u = rms_norm(h, mix_pre_norm[i])
        j = i // N_MIXERS
        if i % N_MIXERS == 0:
            mix = ssd_mixer(u, ssd_w_in[j], ssd_conv_w[j], ssd_conv_b[j], ssd_dt_bias[j],
                            ssd_a_log[j], ssd_d[j], ssd_norm_w[j], ssd_w_out[j])
        else:
            mix = swa_sink_attention(u, attn_w_qkv[j], attn_b_qkv[j], attn_sinks[j],
                                     attn_w_o[j], attn_b_o[j])
        h = h + rms_norm(mix, mix_post_norm[i])
        f = sqrelu_mlp(rms_norm(h, ffn_pre_norm[i]), mlp_w_up[i], mlp_w_down[i])
        h = h + rms_norm(f, ffn_post_norm[i])
    return h
```

```python
import functools

import jax
import jax.numpy as jnp
from jax import lax
from jax.experimental import pallas as pl
from jax.experimental.pallas import tpu as pltpu

F32 = jnp.float32
BF16 = jnp.bfloat16

D_MODEL = 1024
SEQ = 4096
NORM_EPS = 1e-6

LANES = 128
HEAD_DIM = 64

SSD_D_INNER = 2048
SSD_N_HEADS = 32
SSD_N_GROUPS = 8
SSD_D_STATE = 128
SSD_CONV_WIDTH = 4
SSD_CHUNK = 128
SSD_CONV_DIM = SSD_D_INNER + 2 * SSD_N_GROUPS * SSD_D_STATE
SSD_IN_DIM = SSD_D_INNER + SSD_CONV_DIM + SSD_N_HEADS
N_Z_TILES = SSD_D_INNER // LANES
N_X_TILES = SSD_D_INNER // LANES
N_CONV_TILES = SSD_CONV_DIM // LANES
N_IN_TILES = N_Z_TILES + N_CONV_TILES + 1
B_TILE0 = N_X_TILES
C_TILE0 = N_X_TILES + SSD_N_GROUPS
HALO_ROWS = 8

ATTN_N_Q_HEADS = 16
ATTN_N_KV_HEADS = 4
ATTN_REP = 4
ATTN_WINDOW = 128
ATTN_Q_DIM = ATTN_N_Q_HEADS * HEAD_DIM
ATTN_KV_DIM = ATTN_N_KV_HEADS * HEAD_DIM
ATTN_QKV_DIM = ATTN_Q_DIM + 2 * ATTN_KV_DIM

D_FF = 4096
FF_CHUNK = 1024

ROW_TILE = 512
VMEM_LIMIT = 56 * 1024 * 1024


def _rms(x, gain):
    ms = jnp.mean(x * x, axis=-1, keepdims=True)
    return x * lax.rsqrt(ms + NORM_EPS) * gain


def _sigmoid(x):
    return 1.0 / (1.0 + jnp.exp(-x))


def _resident(shape):
    nd = len(shape)
    return pl.BlockSpec(shape, lambda *_: (0,) * nd, pipeline_mode=pl.Buffered(1))


def _params(*sem):
    return pltpu.CompilerParams(dimension_semantics=sem, vmem_limit_bytes=VMEM_LIMIT)


def _ssd_in_proj_kernel(x_ref, g_ref, w_ref, o_ref):
    xn = _rms(x_ref[...], g_ref[...]).astype(BF16)
    pair = 2 * LANES
    for t in range(0, N_IN_TILES - 1, 2):
        acc = jnp.dot(xn, w_ref[:, t * LANES:t * LANES + pair], preferred_element_type=F32)
        o_ref[t] = acc[:, :LANES]
        o_ref[t + 1] = acc[:, LANES:]
    t = N_IN_TILES - 1
    o_ref[t] = jnp.dot(xn, w_ref[:, t * LANES:], preferred_element_type=F32)


def _ssd_in_proj(x2d, gain, w_pad):
    m = x2d.shape[0]
    return pl.pallas_call(
        _ssd_in_proj_kernel,
        grid=(m // ROW_TILE,),
        in_specs=[
            pl.BlockSpec((ROW_TILE, D_MODEL), lambda i: (i, 0)),
            _resident((1, D_MODEL)),
            _resident((D_MODEL, N_IN_TILES * LANES)),
        ],
        out_specs=pl.BlockSpec((N_IN_TILES, ROW_TILE, LANES), lambda i: (0, i, 0)),
        out_shape=jax.ShapeDtypeStruct((N_IN_TILES, m, LANES), F32),
        compiler_params=_params("arbitrary"),
        name="ssd_in_proj",
    )(x2d, gain, w_pad)


def _ssd_kernel(p_ref, cw_ref, cb_ref, dtb_ref, alog_ref, dskip_ref, nw_ref, o_ref,
                xc_ref, halo_ref, state_ref, cumbc_ref, ecumbc_ref, cumt_ref, dtt_ref, wendt_ref,
                dece_ref):
    T = SSD_CHUNK
    lane = lax.broadcasted_iota(jnp.int32, (T, LANES), 1)
    row = lax.broadcasted_iota(jnp.int32, (T, LANES), 0)
    lane_lo = lane < HEAD_DIM
    causal = row >= lane

    @pl.when(pl.program_id(1) == 0)
    def _():
        halo_ref[...] = jnp.zeros_like(halo_ref)
        state_ref[...] = jnp.zeros_like(state_ref)

    dt_in = p_ref[N_IN_TILES - 1] + dtb_ref[...]
    dt = jnp.maximum(dt_in, 0.0) + jnp.log1p(jnp.exp(-jnp.abs(dt_in)))
    a = -jnp.exp(alog_ref[...])
    aq = dt * a
    tri = causal.astype(BF16)
    aq_hi = aq.astype(BF16)
    rem = aq - aq_hi.astype(F32)
    aq_mid = rem.astype(BF16)
    aq_lo = (rem - aq_mid.astype(F32)).astype(BF16)
    cum = (jnp.dot(tri, aq_hi, preferred_element_type=F32)
           + jnp.dot(tri, aq_mid, preferred_element_type=F32)
           + jnp.dot(tri, aq_lo, preferred_element_type=F32))
    cum_last = cum[T - 1:T, :]
    ecum = jnp.exp(cum)
    cumt_ref[...] = cum.T
    dtt_ref[...] = dt.T
    wendt_ref[...] = (dt * jnp.exp(cum_last - cum)).T
    dec = jnp.exp(cum_last)
    for h in range(SSD_N_HEADS):
        cumbc_ref[h] = jnp.broadcast_to(cum[:, h:h + 1], (T, LANES))
        ecumbc_ref[h] = jnp.broadcast_to(ecum[:, h:h + 1], (T, LANES))
    for kx in range(N_X_TILES):
        d0 = jnp.broadcast_to(dec[:, 2 * kx:2 * kx + 1], (1, LANES))
        d1 = jnp.broadcast_to(dec[:, 2 * kx + 1:2 * kx + 2], (1, LANES))
        dece_ref[kx] = jnp.where(lane_lo[0:1], d0, d1)

    def conv_tile(kc, carry):
        raw = p_ref[N_Z_TILES + kc]
        w = cw_ref[kc]
        top = jnp.concatenate([halo_ref[kc], raw[0:HALO_ROWS]], axis=0)
        acc = cb_ref[kc] + w[SSD_CONV_WIDTH - 1:SSD_CONV_WIDTH] * raw
        for d in range(1, SSD_CONV_WIDTH):
            body = pltpu.roll(raw, d, axis=0)
            head = pltpu.roll(top, d, axis=0)[HALO_ROWS:]
            shifted = jnp.concatenate([head, body[HALO_ROWS:]], axis=0)
            acc = acc + w[SSD_CONV_WIDTH - 1 - d:SSD_CONV_WIDTH - d] * shifted
        xc_ref[kc] = acc * _sigmoid(acc)
        halo_ref[kc] = raw[T - HALO_ROWS:]
        return carry

    lax.fori_loop(0, N_CONV_TILES, conv_tile, 0)

    def group(g, carry):
        bm = xc_ref[B_TILE0 + g]
        cm = xc_ref[C_TILE0 + g]
        bt = bm.T
        cb = jnp.dot(cm.astype(BF16), bt.astype(BF16), preferred_element_type=F32)
        gated = []
        for half in range(2):
            kx = 2 * g + half
            x = xc_ref[kx]
            st = state_ref[kx]
            x_lo = jnp.where(lane_lo, x, 0.0).astype(BF16)
            x_hi = jnp.where(lane_lo, 0.0, x).astype(BF16)
            st_lo = jnp.where(lane_lo, st, 0.0).astype(BF16)
            st_hi = jnp.where(lane_lo, 0.0, st).astype(BF16)
            lhs_y = []
            lhs_s = []
            for i in range(2):
                h = 2 * kx + i
                crow = cumt_ref[pl.ds(h, 1), :]
                decay = jnp.where(causal, jnp.exp(cumbc_ref[h] - crow), 0.0)
                lhs_y.append((cb * decay * dtt_ref[pl.ds(h, 1), :]).astype(BF16))
                lhs_y.append((cm * ecumbc_ref[h]).astype(BF16))
                lhs_s.append((bt * wendt_ref[pl.ds(h, 1), :]).astype(BF16))
            y = jnp.dot(jnp.concatenate(lhs_y, axis=1),
                        jnp.concatenate([x_lo, st_lo, x_hi, st_hi], axis=0),
                        preferred_element_type=F32)
            y = y + dskip_ref[kx] * x
            state_ref[kx] = st * dece_ref[kx] + jnp.dot(
                jnp.concatenate(lhs_s, axis=1), jnp.concatenate([x_lo, x_hi], axis=0),
                preferred_element_type=F32)
            z = p_ref[kx]
            gated.append(y * (z * _sigmoid(z)))
        ss = (jnp.sum(gated[0] * gated[0], axis=-1, keepdims=True)
              + jnp.sum(gated[1] * gated[1], axis=-1, keepdims=True))
        r = lax.rsqrt(ss * (1.0 / (2 * LANES)) + NORM_EPS)
        for half in range(2):
            kx = 2 * g + half
            o_ref[kx] = (gated[half] * r * nw_ref[kx]).astype(o_ref.dtype)
        return carry

    lax.fori_loop(0, SSD_N_GROUPS, group, 0)


def _ssd_core(p, conv_w, conv_b, dt_bias, a_log, dskip, norm_w, batch):
    m = p.shape[1]
    n_chunks = m // batch // SSD_CHUNK
    T = SSD_CHUNK
    return pl.pallas_call(
        _ssd_kernel,
        grid=(batch, n_chunks),
        in_specs=[
            pl.BlockSpec((N_IN_TILES, T, LANES), lambda b, c: (0, b * n_chunks + c, 0)),
            _resident((N_CONV_TILES, SSD_CONV_WIDTH, LANES)),
            _resident((N_CONV_TILES, 1, LANES)),
            _resident((1, LANES)),
            _resident((1, LANES)),
            _resident((N_X_TILES, 1, LANES)),
            _resident((N_X_TILES, 1, LANES)),
        ],
        out_specs=pl.BlockSpec((N_X_TILES, T, LANES), lambda b, c: (0, b * n_chunks + c, 0)),
        out_shape=jax.ShapeDtypeStruct((N_X_TILES, m, LANES), BF16),
        scratch_shapes=[
            pltpu.VMEM((N_CONV_TILES, T, LANES), F32),
            pltpu.VMEM((N_CONV_TILES, HALO_ROWS, LANES), F32),
            pltpu.VMEM((N_X_TILES, SSD_D_STATE, LANES), F32),
            pltpu.VMEM((SSD_N_HEADS, T, LANES), F32),
            pltpu.VMEM((SSD_N_HEADS, T, LANES), F32),
            pltpu.VMEM((LANES, T), F32),
            pltpu.VMEM((LANES, T), F32),
            pltpu.VMEM((LANES, T), F32),
            pltpu.VMEM((N_X_TILES, 1, LANES), F32),
        ],
        compiler_params=_params("arbitrary", "arbitrary"),
        name="ssd_core",
    )(p, conv_w, conv_b, dt_bias, a_log, dskip, norm_w)


def _tiles_proj_res_kernel(y_ref, w_ref, h_ref, g_ref, o_ref):
    y = jnp.concatenate([y_ref[k] for k in range(y_ref.shape[0])], axis=1)
    mix = jnp.dot(y, w_ref[...], preferred_element_type=F32)
    o_ref[...] = h_ref[...] + _rms(mix, g_ref[...])


def _ssd_out_proj(y_tiles, w, h, gain):
    m = h.shape[0]
    n_t = y_tiles.shape[0]
    return pl.pallas_call(
        _tiles_proj_res_kernel,
        grid=(m // ROW_TILE,),
        in_specs=[
            pl.BlockSpec((n_t, ROW_TILE, LANES), lambda i: (0, i, 0)),
            _resident(w.shape),
            pl.BlockSpec((ROW_TILE, D_MODEL), lambda i: (i, 0)),
            _resident((1, D_MODEL)),
        ],
        out_specs=pl.BlockSpec((ROW_TILE, D_MODEL), lambda i: (i, 0)),
        out_shape=jax.ShapeDtypeStruct((m, D_MODEL), F32),
        compiler_params=_params("arbitrary"),
        name="ssd_out_proj",
    )(y_tiles, w, h, gain)


def _proj_bias_res_kernel(a_ref, w_ref, b_ref, h_ref, g_ref, o_ref):
    mix = jnp.dot(a_ref[...], w_ref[...], preferred_element_type=F32) + b_ref[...]
    o_ref[...] = h_ref[...] + _rms(mix, g_ref[...])


def _attn_out_proj(a, w, b, h, gain):
    m = h.shape[0]
    return pl.pallas_call(
        _proj_bias_res_kernel,
        grid=(m // ROW_TILE,),
        in_specs=[
            pl.BlockSpec((ROW_TILE, a.shape[1]), lambda i: (i, 0)),
            _resident(w.shape),
            _resident((1, D_MODEL)),
            pl.BlockSpec((ROW_TILE, D_MODEL), lambda i: (i, 0)),
            _resident((1, D_MODEL)),
        ],
        out_specs=pl.BlockSpec((ROW_TILE, D_MODEL), lambda i: (i, 0)),
        out_shape=jax.ShapeDtypeStruct((m, D_MODEL), F32),
        compiler_params=_params("arbitrary"),
        name="attn_out_proj",
    )(a, w, b, h, gain)


def _mlp_kernel(h_ref, g1_ref, wu_ref, wd_ref, g2_ref, o_ref):
    h = h_ref[...]
    u = _rms(h, g1_ref[...]).astype(BF16)
    acc = None
    for c in range(0, D_FF, FF_CHUNK):
        a = jnp.dot(u, wu_ref[:, c:c + FF_CHUNK], preferred_element_type=F32)
        a = jnp.square(jnp.maximum(a, 0.0)).astype(BF16)
        part = jnp.dot(a, wd_ref[c:c + FF_CHUNK, :], preferred_element_type=F32)
        acc = part if acc is None else acc + part
    o_ref[...] = h + _rms(acc, g2_ref[...])


def _mlp(h, g1, wu, wd, g2):
    m = h.shape[0]
    return pl.pallas_call(
        _mlp_kernel,
        grid=(m // ROW_TILE,),
        in_specs=[
            pl.BlockSpec((ROW_TILE, D_MODEL), lambda i: (i, 0)),
            _resident((1, D_MODEL)),
            _resident((D_MODEL, D_FF)),
            _resident((D_FF, D_MODEL)),
            _resident((1, D_MODEL)),
        ],
        out_specs=pl.BlockSpec((ROW_TILE, D_MODEL), lambda i: (i, 0)),
        out_shape=jax.ShapeDtypeStruct((m, D_MODEL), F32),
        compiler_params=_params("arbitrary"),
        name="sqrelu_mlp",
    )(h, g1, wu, wd, g2)


def _qkv_kernel(x_ref, g_ref, w_ref, b_ref, o_ref):
    xn = _rms(x_ref[...], g_ref[...]).astype(BF16)
    acc = jnp.dot(xn, w_ref[...], preferred_element_type=F32) + b_ref[...]
    o_ref[...] = acc.astype(o_ref.dtype)


def _qkv_proj(h, gain, w, b):
    m = h.shape[0]
    return pl.pallas_call(
        _qkv_kernel,
        grid=(m // ROW_TILE,),
        in_specs=[
            pl.BlockSpec((ROW_TILE, D_MODEL), lambda i: (i, 0)),
            _resident((1, D_MODEL)),
            _resident((D_MODEL, ATTN_QKV_DIM)),
            _resident((1, ATTN_QKV_DIM)),
        ],
        out_specs=pl.BlockSpec((ROW_TILE, ATTN_QKV_DIM), lambda i: (i, 0)),
        out_shape=jax.ShapeDtypeStruct((m, ATTN_QKV_DIM), BF16),
        compiler_params=_params("arbitrary"),
        name="attn_qkv_proj",
    )(h, gain, w, b)


def _attn_kernel(sink_ref, q_ref, kp_ref, kc_ref, vp_ref, vc_ref, o_ref):
    W = ATTN_WINDOW
    kmin = jnp.where(pl.program_id(1) == 0, W, 0)
    kcat = jnp.concatenate([kp_ref[...], kc_ref[...]], axis=0)
    vcat = jnp.concatenate([vp_ref[...], vc_ref[...]], axis=0)
    qrow = lax.broadcasted_iota(jnp.int32, (2 * W, 2 * W), 0)
    kpos = lax.broadcasted_iota(jnp.int32, (2 * W, 2 * W), 1)
    qpos = jnp.where(qrow >= W, qrow - W, qrow) + W
    rel = qpos - kpos
    valid = (rel >= 0) & (rel < W) & (kpos >= kmin)
    lane = lax.broadcasted_iota(jnp.int32, (W, LANES), 1)
    lane_lo = lane < HEAD_DIM
    is_lo_row = lax.broadcasted_iota(jnp.int32, (2 * W, 1), 0) < W
    zero = jnp.zeros((), BF16)
    for t in range(ATTN_Q_DIM // LANES):
        pair = t // ATTN_REP
        qt = q_ref[:, t * LANES:(t + 1) * LANES] * jnp.asarray(HEAD_DIM ** -0.5, BF16)
        qs = jnp.concatenate([jnp.where(lane_lo, qt, zero), jnp.where(lane_lo, zero, qt)], axis=0)
        kt = kcat[:, pair * LANES:(pair + 1) * LANES]
        vt = vcat[:, pair * LANES:(pair + 1) * LANES]
        s = lax.dot_general(qs, kt, (((1,), (1,)), ((), ())), preferred_element_type=F32)
        s = jnp.where(valid, s, -jnp.inf)
        sink = jnp.where(is_lo_row, sink_ref[2 * t], sink_ref[2 * t + 1])
        mx = jnp.maximum(jnp.max(s, axis=-1, keepdims=True), sink)
        e = jnp.exp(s - mx)
        den = jnp.sum(e, axis=-1, keepdims=True) + jnp.exp(sink - mx)
        p = (e * (1.0 / den)).astype(BF16)
        pv = jnp.dot(p, vt, preferred_element_type=F32)
        o_ref[:, t * LANES:(t + 1) * LANES] = jnp.where(lane_lo, pv[:W], pv[W:]).astype(o_ref.dtype)


def _attention(qkv, sinks, batch):
    m = qkv.shape[0]
    nb = m // batch // ATTN_WINDOW
    W = ATTN_WINDOW
    k_col = ATTN_Q_DIM // ATTN_KV_DIM
    v_col = k_col + 1

    def cur(b, i):
        return b * nb + i

    def prev(b, i):
        return jnp.maximum(b * nb + i - 1, 0)

    return pl.pallas_call(
        _attn_kernel,
        grid=(batch, nb),
        in_specs=[
            pl.BlockSpec(memory_space=pltpu.SMEM),
            pl.BlockSpec((W, ATTN_Q_DIM), lambda b, i: (cur(b, i), 0)),
            pl.BlockSpec((W, ATTN_KV_DIM), lambda b, i: (prev(b, i), k_col)),
            pl.BlockSpec((W, ATTN_KV_DIM), lambda b, i: (cur(b, i), k_col)),
            pl.BlockSpec((W, ATTN_KV_DIM), lambda b, i: (prev(b, i), v_col)),
            pl.BlockSpec((W, ATTN_KV_DIM), lambda b, i: (cur(b, i), v_col)),
        ],
        out_specs=pl.BlockSpec((W, ATTN_Q_DIM), lambda b, i: (cur(b, i), 0)),
        out_shape=jax.ShapeDtypeStruct((m, ATTN_Q_DIM), BF16),
        compiler_params=_params("arbitrary", "arbitrary"),
        name="swa_sink_attention",
    )(sinks, qkv, qkv, qkv, qkv, qkv)


def _attn_q_perm():
    cols = []
    heads = []
    for pair in range(ATTN_N_KV_HEADS // 2):
        for r in range(ATTN_REP):
            for kv in (2 * pair, 2 * pair + 1):
                head = kv * ATTN_REP + r
                heads.append(head)
                cols.extend(range(head * HEAD_DIM, (head + 1) * HEAD_DIM))
    return jnp.asarray(cols, jnp.int32), jnp.asarray(heads, jnp.int32)


def _lane_tiles(v, n_tiles):
    return v.astype(F32).reshape(n_tiles, 1, LANES)


def kernel(x, ssd_w_in, ssd_conv_w, ssd_conv_b, ssd_dt_bias, ssd_a_log, ssd_d, ssd_norm_w, ssd_w_out, attn_w_qkv, attn_b_qkv, attn_sinks, attn_w_o, attn_b_o, mlp_w_up, mlp_w_down, mix_pre_norm, mix_post_norm, ffn_pre_norm, ffn_post_norm):
    batch, seq, d = x.shape
    m = batch * seq
    h = x.reshape(m, d)
    row = lambda v: v.astype(F32).reshape(1, -1)

    w_in = jnp.pad(ssd_w_in[0], ((0, 0), (0, N_IN_TILES * LANES - SSD_IN_DIM))).astype(BF16)
    p = _ssd_in_proj(h, row(mix_pre_norm[0]), w_in)
    conv_w = ssd_conv_w[0].astype(F32).reshape(SSD_CONV_WIDTH, N_CONV_TILES, LANES).transpose(1, 0, 2)
    pad_heads = lambda v: jnp.pad(v.astype(F32), (0, LANES - SSD_N_HEADS)).reshape(1, LANES)
    y = _ssd_core(
        p, conv_w, _lane_tiles(ssd_conv_b[0], N_CONV_TILES), pad_heads(ssd_dt_bias[0]),
        pad_heads(ssd_a_log[0]), _lane_tiles(jnp.repeat(ssd_d[0], HEAD_DIM), N_X_TILES),
        _lane_tiles(ssd_norm_w[0], N_X_TILES), batch)
    h = _ssd_out_proj(y, ssd_w_out[0].astype(BF16), h, row(mix_post_norm[0]))
    h = _mlp(h, row(ffn_pre_norm[0]), mlp_w_up[0].astype(BF16), mlp_w_down[0].astype(BF16),
             row(ffn_post_norm[0]))

    q_cols, q_heads = _attn_q_perm()
    kv_cols = jnp.arange(ATTN_Q_DIM, ATTN_QKV_DIM, dtype=jnp.int32)
    qkv_cols = jnp.concatenate([q_cols, kv_cols])
    w_qkv = attn_w_qkv[0][:, qkv_cols].astype(BF16)
    b_qkv = attn_b_qkv[0][qkv_cols].astype(F32).reshape(1, -1)
    qkv = _qkv_proj(h, row(mix_pre_norm[1]), w_qkv, b_qkv)
    a = _attention(qkv, attn_sinks[0][q_heads].astype(F32), batch)
    h = _attn_out_proj(a, attn_w_o[0][q_cols, :].astype(BF16), row(attn_b_o[0]), h,
                       row(mix_post_norm[1]))
    h = _mlp(h, row(ffn_pre_norm[1]), mlp_w_up[1].astype(BF16), mlp_w_down[1].astype(BF16),
             row(ffn_post_norm[1]))
    return h.reshape(batch, seq, d)
```

```python
import functools

import jax
import jax.numpy as jnp
from jax import lax
from jax.experimental import pallas as pl
from jax.experimental.pallas import tpu as pltpu

F32 = jnp.float32
BF16 = jnp.bfloat16

D_MODEL = 1024
SEQ = 4096
NORM_EPS = 1e-6
LOG2_E = 1.4426950408889634

LANES = 128
HEAD_DIM = 64

SSD_D_INNER = 2048
SSD_N_HEADS = 32
SSD_N_GROUPS = 8
SSD_D_STATE = 128
SSD_CONV_WIDTH = 4
SSD_CHUNK = 128
SSD_CONV_DIM = SSD_D_INNER + 2 * SSD_N_GROUPS * SSD_D_STATE
SSD_IN_DIM = SSD_D_INNER + SSD_CONV_DIM + SSD_N_HEADS
N_Z_TILES = SSD_D_INNER // LANES
N_X_TILES = SSD_D_INNER // LANES
N_CONV_TILES = SSD_CONV_DIM // LANES
N_IN_TILES = N_Z_TILES + N_CONV_TILES + 1
B_TILE0 = N_X_TILES
C_TILE0 = N_X_TILES + SSD_N_GROUPS
HALO_ROWS = 8

ATTN_N_Q_HEADS = 16
ATTN_N_KV_HEADS = 4
ATTN_REP = 4
ATTN_WINDOW = 128
ATTN_Q_DIM = ATTN_N_Q_HEADS * HEAD_DIM
ATTN_KV_DIM = ATTN_N_KV_HEADS * HEAD_DIM
ATTN_QKV_DIM = ATTN_Q_DIM + 2 * ATTN_KV_DIM

D_FF = 4096
FF_CHUNK = 1024

ROW_TILE = 512
VMEM_LIMIT = 56 * 1024 * 1024


def _rms(x, gain):
    ms = jnp.mean(x * x, axis=-1, keepdims=True)
    return x * lax.rsqrt(ms + NORM_EPS) * gain


def _sigmoid(x):
    return 1.0 / (1.0 + jnp.exp2(x * (-LOG2_E)))


def _resident(shape):
    nd = len(shape)
    return pl.BlockSpec(shape, lambda *_: (0,) * nd, pipeline_mode=pl.Buffered(1))


def _params(*sem):
    return pltpu.CompilerParams(dimension_semantics=sem, vmem_limit_bytes=VMEM_LIMIT)


def _ssd_in_proj_kernel(x_ref, g_ref, w_ref, o_ref):
    xn = _rms(x_ref[...], g_ref[...]).astype(BF16)
    pair = 2 * LANES
    for t in range(0, N_IN_TILES - 1, 2):
        acc = jnp.dot(xn, w_ref[:, t * LANES:t * LANES + pair], preferred_element_type=F32)
        o_ref[t] = acc[:, :LANES]
        o_ref[t + 1] = acc[:, LANES:]
    t = N_IN_TILES - 1
    o_ref[t] = jnp.dot(xn, w_ref[:, t * LANES:], preferred_element_type=F32)


def _ssd_in_proj(x2d, gain, w_pad):
    m = x2d.shape[0]
    return pl.pallas_call(
        _ssd_in_proj_kernel,
        grid=(m // ROW_TILE,),
        in_specs=[
            pl.BlockSpec((ROW_TILE, D_MODEL), lambda i: (i, 0)),
            _resident((1, D_MODEL)),
            _resident((D_MODEL, N_IN_TILES * LANES)),
        ],
        out_specs=pl.BlockSpec((N_IN_TILES, ROW_TILE, LANES), lambda i: (0, i, 0)),
        out_shape=jax.ShapeDtypeStruct((N_IN_TILES, m, LANES), F32),
        compiler_params=_params("arbitrary"),
        name="ssd_in_proj",
    )(x2d, gain, w_pad)


def _ssd_kernel(p_ref, cw_ref, cb_ref, dtb_ref, alog_ref, dskip_ref, nw_ref, o_ref,
                xc_ref, halo_ref, state_ref, cum_ref, cumt_ref, dtt_ref, wendt_ref, dece_ref):
    T = SSD_CHUNK
    lane = lax.broadcasted_iota(jnp.int32, (T, LANES), 1)
    row = lax.broadcasted_iota(jnp.int32, (T, LANES), 0)
    lane_lo = lane < HEAD_DIM
    causal = row >= lane

    @pl.when(pl.program_id(1) == 0)
    def _():
        halo_ref[...] = jnp.zeros_like(halo_ref)
        state_ref[...] = jnp.zeros_like(state_ref)

    dt_in = p_ref[N_IN_TILES - 1] + dtb_ref[...]
    dt = jnp.maximum(dt_in, 0.0) + jnp.log1p(jnp.exp(-jnp.abs(dt_in)))
    a = -jnp.exp(alog_ref[...])
    aq = dt * a
    tri = causal.astype(BF16)
    aq_hi = aq.astype(BF16)
    rem = aq - aq_hi.astype(F32)
    aq_mid = rem.astype(BF16)
    aq_lo = (rem - aq_mid.astype(F32)).astype(BF16)
    cum = (jnp.dot(tri, aq_hi, preferred_element_type=F32)
           + jnp.dot(tri, aq_mid, preferred_element_type=F32)
           + jnp.dot(tri, aq_lo, preferred_element_type=F32))
    cum_last = cum[T - 1:T, :]
    cum2 = cum * LOG2_E
    cum_ref[...] = cum2
    cumt_ref[...] = cum2.T
    dtt_ref[...] = dt.T
    wendt_ref[...] = (dt * jnp.exp(cum_last - cum)).T
    dec = jnp.exp(cum_last)
    for kx in range(N_X_TILES):
        d0 = jnp.broadcast_to(dec[:, 2 * kx:2 * kx + 1], (1, LANES))
        d1 = jnp.broadcast_to(dec[:, 2 * kx + 1:2 * kx + 2], (1, LANES))
        dece_ref[kx] = jnp.where(lane_lo[0:1], d0, d1)

    def conv_tile(kc, carry):
        raw = p_ref[N_Z_TILES + kc]
        w = cw_ref[kc]
        top = jnp.concatenate([halo_ref[kc], raw[0:HALO_ROWS]], axis=0)
        acc = cb_ref[kc] + w[SSD_CONV_WIDTH - 1:SSD_CONV_WIDTH] * raw
        for d in range(1, SSD_CONV_WIDTH):
            body = pltpu.roll(raw, d, axis=0)
            head = pltpu.roll(top, d, axis=0)[HALO_ROWS:]
            shifted = jnp.concatenate([head, body[HALO_ROWS:]], axis=0)
            acc = acc + w[SSD_CONV_WIDTH - 1 - d:SSD_CONV_WIDTH - d] * shifted
        xc_ref[kc] = acc * _sigmoid(acc)
        halo_ref[kc] = raw[T - HALO_ROWS:]
        return carry

    for kc in range(N_CONV_TILES):
        conv_tile(kc, 0)

    def group(g, carry):
        bm = xc_ref[B_TILE0 + g]
        cm = xc_ref[C_TILE0 + g]
        bt = bm.T
        cb = jnp.dot(cm.astype(BF16), bt.astype(BF16), preferred_element_type=F32)
        gated = []
        for half in range(2):
            kx = 2 * g + half
            x = xc_ref[kx]
            st = state_ref[kx]
            x_lo = jnp.where(lane_lo, x, 0.0).astype(BF16)
            x_hi = jnp.where(lane_lo, 0.0, x).astype(BF16)
            st_lo = jnp.where(lane_lo, st, 0.0).astype(BF16)
            st_hi = jnp.where(lane_lo, 0.0, st).astype(BF16)
            lhs_y = []
            lhs_s = []
            for i in range(2):
                h = 2 * kx + i
                crow = cumt_ref[pl.ds(h, 1), :]
                cbc = jnp.broadcast_to(cum_ref[:, h:h + 1], (T, LANES))
                decay = jnp.where(causal, jnp.exp2(cbc - crow), 0.0)
                lhs_y.append((cb * decay * dtt_ref[pl.ds(h, 1), :]).astype(BF16))
                lhs_y.append((cm * jnp.exp2(cbc)).astype(BF16))
                lhs_s.append((bt * wendt_ref[pl.ds(h, 1), :]).astype(BF16))
            y = jnp.dot(jnp.concatenate(lhs_y, axis=1),
                        jnp.concatenate([x_lo, st_lo, x_hi, st_hi], axis=0),
                        preferred_element_type=F32)
            y = y + dskip_ref[kx] * x
            state_ref[kx] = st * dece_ref[kx] + jnp.dot(
                jnp.concatenate(lhs_s, axis=1), jnp.concatenate([x_lo, x_hi], axis=0),
                preferred_element_type=F32)
            z = p_ref[kx]
            gated.append(y * (z * _sigmoid(z)))
        ss = (jnp.sum(gated[0] * gated[0], axis=-1, keepdims=True)
              + jnp.sum(gated[1] * gated[1], axis=-1, keepdims=True))
        r = lax.rsqrt(ss * (1.0 / (2 * LANES)) + NORM_EPS)
        for half in range(2):
            kx = 2 * g + half
            o_ref[kx] = (gated[half] * r * nw_ref[kx]).astype(o_ref.dtype)
        return carry

    for g in range(SSD_N_GROUPS):
        group(g, 0)


def _ssd_core(p, conv_w, conv_b, dt_bias, a_log, dskip, norm_w, batch):
    m = p.shape[1]
    n_chunks = m // batch // SSD_CHUNK
    T = SSD_CHUNK
    return pl.pallas_call(
        _ssd_kernel,
        grid=(batch, n_chunks),
        in_specs=[
            pl.BlockSpec((N_IN_TILES, T, LANES), lambda b, c: (0, b * n_chunks + c, 0)),
            _resident((N_CONV_TILES, SSD_CONV_WIDTH, LANES)),
            _resident((N_CONV_TILES, 1, LANES)),
            _resident((1, LANES)),
            _resident((1, LANES)),
            _resident((N_X_TILES, 1, LANES)),
            _resident((N_X_TILES, 1, LANES)),
        ],
        out_specs=pl.BlockSpec((N_X_TILES, T, LANES), lambda b, c: (0, b * n_chunks + c, 0)),
        out_shape=jax.ShapeDtypeStruct((N_X_TILES, m, LANES), BF16),
        scratch_shapes=[
            pltpu.VMEM((N_CONV_TILES, T, LANES), F32),
            pltpu.VMEM((N_CONV_TILES, HALO_ROWS, LANES), F32),
            pltpu.VMEM((N_X_TILES, SSD_D_STATE, LANES), F32),
            pltpu.VMEM((T, LANES), F32),
            pltpu.VMEM((LANES, T), F32),
            pltpu.VMEM((LANES, T), F32),
            pltpu.VMEM((LANES, T), F32),
            pltpu.VMEM((N_X_TILES, 1, LANES), F32),
        ],
        compiler_params=_params("arbitrary", "arbitrary"),
        name="ssd_core",
    )(p, conv_w, conv_b, dt_bias, a_log, dskip, norm_w)


def _tiles_proj_res_kernel(y_ref, w_ref, h_ref, g_ref, o_ref):
    y = jnp.concatenate([y_ref[k] for k in range(y_ref.shape[0])], axis=1)
    mix = jnp.dot(y, w_ref[...], preferred_element_type=F32)
    o_ref[...] = h_ref[...] + _rms(mix, g_ref[...])


def _ssd_out_proj(y_tiles, w, h, gain):
    m = h.shape[0]
    n_t = y_tiles.shape[0]
    return pl.pallas_call(
        _tiles_proj_res_kernel,
        grid=(m // ROW_TILE,),
        in_specs=[
            pl.BlockSpec((n_t, ROW_TILE, LANES), lambda i: (0, i, 0)),
            _resident(w.shape),
            pl.BlockSpec((ROW_TILE, D_MODEL), lambda i: (i, 0)),
            _resident((1, D_MODEL)),
        ],
        out_specs=pl.BlockSpec((ROW_TILE, D_MODEL), lambda i: (i, 0)),
        out_shape=jax.ShapeDtypeStruct((m, D_MODEL), F32),
        compiler_params=_params("arbitrary"),
        name="ssd_out_proj",
    )(y_tiles, w, h, gain)


def _proj_bias_res_kernel(a_ref, w_ref, b_ref, h_ref, g_ref, o_ref):
    mix = jnp.dot(a_ref[...], w_ref[...], preferred_element_type=F32) + b_ref[...]
    o_ref[...] = h_ref[...] + _rms(mix, g_ref[...])


def _attn_out_proj(a, w, b, h, gain):
    m = h.shape[0]
    return pl.pallas_call(
        _proj_bias_res_kernel,
        grid=(m // ROW_TILE,),
        in_specs=[
            pl.BlockSpec((ROW_TILE, a.shape[1]), lambda i: (i, 0)),
            _resident(w.shape),
            _resident((1, D_MODEL)),
            pl.BlockSpec((ROW_TILE, D_MODEL), lambda i: (i, 0)),
            _resident((1, D_MODEL)),
        ],
        out_specs=pl.BlockSpec((ROW_TILE, D_MODEL), lambda i: (i, 0)),
        out_shape=jax.ShapeDtypeStruct((m, D_MODEL), F32),
        compiler_params=_params("arbitrary"),
        name="attn_out_proj",
    )(a, w, b, h, gain)


def _mlp_kernel(h_ref, g1_ref, wu_ref, wd_ref, g2_ref, o_ref):
    h = h_ref[...]
    u = _rms(h, g1_ref[...]).astype(BF16)
    acc = None
    for c in range(0, D_FF, FF_CHUNK):
        a = jnp.dot(u, wu_ref[:, c:c + FF_CHUNK], preferred_element_type=F32)
        a = jnp.square(jnp.maximum(a, 0.0)).astype(BF16)
        part = jnp.dot(a, wd_ref[c:c + FF_CHUNK, :], preferred_element_type=F32)
        acc = part if acc is None else acc + part
    o_ref[...] = h + _rms(acc, g2_ref[...])


def _mlp(h, g1, wu, wd, g2):
    m = h.shape[0]
    return pl.pallas_call(
        _mlp_kernel,
        grid=(m // ROW_TILE,),
        in_specs=[
            pl.BlockSpec((ROW_TILE, D_MODEL), lambda i: (i, 0)),
            _resident((1, D_MODEL)),
            _resident((D_MODEL, D_FF)),
            _resident((D_FF, D_MODEL)),
            _resident((1, D_MODEL)),
        ],
        out_specs=pl.BlockSpec((ROW_TILE, D_MODEL), lambda i: (i, 0)),
        out_shape=jax.ShapeDtypeStruct((m, D_MODEL), F32),
        compiler_params=_params("arbitrary"),
        name="sqrelu_mlp",
    )(h, g1, wu, wd, g2)


def _qkv_kernel(x_ref, g_ref, w_ref, b_ref, o_ref):
    xn = _rms(x_ref[...], g_ref[...]).astype(BF16)
    acc = jnp.dot(xn, w_ref[...], preferred_element_type=F32) + b_ref[...]
    o_ref[...] = acc.astype(o_ref.dtype)


def _qkv_proj(h, gain, w, b):
    m = h.shape[0]
    return pl.pallas_call(
        _qkv_kernel,
        grid=(m // ROW_TILE,),
        in_specs=[
            pl.BlockSpec((ROW_TILE, D_MODEL), lambda i: (i, 0)),
            _resident((1, D_MODEL)),
            _resident((D_MODEL, ATTN_QKV_DIM)),
            _resident((1, ATTN_QKV_DIM)),
        ],
        out_specs=pl.BlockSpec((ROW_TILE, ATTN_QKV_DIM), lambda i: (i, 0)),
        out_shape=jax.ShapeDtypeStruct((m, ATTN_QKV_DIM), BF16),
        compiler_params=_params("arbitrary"),
        name="attn_qkv_proj",
    )(h, gain, w, b)


def _attn_kernel(sink_ref, q_ref, kp_ref, kc_ref, vp_ref, vc_ref, o_ref):
    W = ATTN_WINDOW
    kmin = jnp.where(pl.program_id(1) == 0, W, 0)
    kcat = jnp.concatenate([kp_ref[...], kc_ref[...]], axis=0)
    vcat = jnp.concatenate([vp_ref[...], vc_ref[...]], axis=0)
    qrow = lax.broadcasted_iota(jnp.int32, (2 * W, 2 * W), 0)
    kpos = lax.broadcasted_iota(jnp.int32, (2 * W, 2 * W), 1)
    qpos = jnp.where(qrow >= W, qrow - W, qrow) + W
    rel = qpos - kpos
    valid = (rel >= 0) & (rel < W) & (kpos >= kmin)
    lane = lax.broadcasted_iota(jnp.int32, (W, LANES), 1)
    lane_lo = lane < HEAD_DIM
    is_lo_row = lax.broadcasted_iota(jnp.int32, (2 * W, 1), 0) < W
    zero = jnp.zeros((), BF16)
    for t in range(ATTN_Q_DIM // LANES):
        pair = t // ATTN_REP
        qt = q_ref[:, t * LANES:(t + 1) * LANES] * jnp.asarray(HEAD_DIM ** -0.5, BF16)
        qs = jnp.concatenate([jnp.where(lane_lo, qt, zero), jnp.where(lane_lo, zero, qt)], axis=0)
        kt = kcat[:, pair * LANES:(pair + 1) * LANES]
        vt = vcat[:, pair * LANES:(pair + 1) * LANES]
        s = lax.dot_general(qs, kt, (((1,), (1,)), ((), ())), preferred_element_type=F32)
        s = jnp.where(valid, s, -jnp.inf)
        sink = jnp.where(is_lo_row, sink_ref[2 * t], sink_ref[2 * t + 1])
        mx = jnp.maximum(jnp.max(s, axis=-1, keepdims=True), sink)
        e = jnp.exp(s - mx)
        den = jnp.sum(e, axis=-1, keepdims=True) + jnp.exp(sink - mx)
        pv = jnp.dot(e.astype(BF16), vt, preferred_element_type=F32) * (1.0 / den)
        o_ref[:, t * LANES:(t + 1) * LANES] = jnp.where(lane_lo, pv[:W], pv[W:]).astype(o_ref.dtype)


def _attention(qkv, sinks, batch):
    m = qkv.shape[0]
    nb = m // batch // ATTN_WINDOW
    W = ATTN_WINDOW
    k_col = ATTN_Q_DIM // ATTN_KV_DIM
    v_col = k_col + 1

    def cur(b, i):
        return b * nb + i

    def prev(b, i):
        return jnp.maximum(b * nb + i - 1, 0)

    return pl.pallas_call(
        _attn_kernel,
        grid=(batch, nb),
        in_specs=[
            pl.BlockSpec(memory_space=pltpu.SMEM),
            pl.BlockSpec((W, ATTN_Q_DIM), lambda b, i: (cur(b, i), 0)),
            pl.BlockSpec((W, ATTN_KV_DIM), lambda b, i: (prev(b, i), k_col)),
            pl.BlockSpec((W, ATTN_KV_DIM), lambda b, i: (cur(b, i), k_col)),
            pl.BlockSpec((W, ATTN_KV_DIM), lambda b, i: (prev(b, i), v_col)),
            pl.BlockSpec((W, ATTN_KV_DIM), lambda b, i: (cur(b, i), v_col)),
        ],
        out_specs=pl.BlockSpec((W, ATTN_Q_DIM), lambda b, i: (cur(b, i), 0)),
        out_shape=jax.ShapeDtypeStruct((m, ATTN_Q_DIM), BF16),
        compiler_params=_params("arbitrary", "arbitrary"),
        name="swa_sink_attention",
    )(sinks, qkv, qkv, qkv, qkv, qkv)


def _permute_q_heads(v, axis):
    inner = v.shape[axis] // ATTN_N_Q_HEADS
    shape = v.shape[:axis] + (ATTN_N_KV_HEADS // 2, 2, ATTN_REP, inner) + v.shape[axis + 1:]
    order = list(range(len(shape)))
    order[axis + 1], order[axis + 2] = axis + 2, axis + 1
    return v.reshape(shape).transpose(order).reshape(v.shape)


def _lane_tiles(v, n_tiles):
    return v.astype(F32).reshape(n_tiles, 1, LANES)


def kernel(x, ssd_w_in, ssd_conv_w, ssd_conv_b, ssd_dt_bias, ssd_a_log, ssd_d, ssd_norm_w, ssd_w_out, attn_w_qkv, attn_b_qkv, attn_sinks, attn_w_o, attn_b_o, mlp_w_up, mlp_w_down, mix_pre_norm, mix_post_norm, ffn_pre_norm, ffn_post_norm):
    batch, seq, d = x.shape
    m = batch * seq
    h = x.reshape(m, d)
    row = lambda v: v.astype(F32).reshape(1, -1)

    w_in = jnp.pad(ssd_w_in[0], ((0, 0), (0, N_IN_TILES * LANES - SSD_IN_DIM))).astype(BF16)
    p = _ssd_in_proj(h, row(mix_pre_norm[0]), w_in)
    conv_w = ssd_conv_w[0].astype(F32).reshape(SSD_CONV_WIDTH, N_CONV_TILES, LANES).transpose(1, 0, 2)
    pad_heads = lambda v: jnp.pad(v.astype(F32), (0, LANES - SSD_N_HEADS)).reshape(1, LANES)
    y = _ssd_core(
        p, conv_w, _lane_tiles(ssd_conv_b[0], N_CONV_TILES), pad_heads(ssd_dt_bias[0]),
        pad_heads(ssd_a_log[0]), _lane_tiles(jnp.repeat(ssd_d[0], HEAD_DIM), N_X_TILES),
        _lane_tiles(ssd_norm_w[0], N_X_TILES), batch)
    h = _ssd_out_proj(y, ssd_w_out[0].astype(BF16), h, row(mix_post_norm[0]))
    h = _mlp(h, row(ffn_pre_norm[0]), mlp_w_up[0].astype(BF16), mlp_w_down[0].astype(BF16),
             row(ffn_post_norm[0]))

    wq, wkv = attn_w_qkv[0][:, :ATTN_Q_DIM], attn_w_qkv[0][:, ATTN_Q_DIM:]
    bq, bkv = attn_b_qkv[0][:ATTN_Q_DIM], attn_b_qkv[0][ATTN_Q_DIM:]
    w_qkv = jnp.concatenate([_permute_q_heads(wq, 1), wkv], axis=1).astype(BF16)
    b_qkv = jnp.concatenate([_permute_q_heads(bq, 0), bkv]).astype(F32).reshape(1, -1)
    qkv = _qkv_proj(h, row(mix_pre_norm[1]), w_qkv, b_qkv)
    a = _attention(qkv, _permute_q_heads(attn_sinks[0], 0).astype(F32), batch)
    h = _attn_out_proj(a, _permute_q_heads(attn_w_o[0], 0).astype(BF16), row(attn_b_o[0]), h,
                       row(mix_post_norm[1]))
    h = _mlp(h, row(ffn_pre_norm[1]), mlp_w_up[1].astype(BF16), mlp_w_down[1].astype(BF16),
             row(ffn_post_norm[1]))
    return h.reshape(batch, seq, d)
```

```python
import functools

import jax
import jax.numpy as jnp
from jax import lax
from jax.experimental import pallas as pl
from jax.experimental.pallas import tpu as pltpu

F32 = jnp.float32
BF16 = jnp.bfloat16

D_MODEL = 1024
SEQ = 4096
NORM_EPS = 1e-6
LOG2_E = 1.4426950408889634

LANES = 128
HEAD_DIM = 64

SSD_D_INNER = 2048
SSD_N_HEADS = 32
SSD_N_GROUPS = 8
SSD_D_STATE = 128
SSD_CONV_WIDTH = 4
SSD_CHUNK = 128
SSD_CONV_DIM = SSD_D_INNER + 2 * SSD_N_GROUPS * SSD_D_STATE
SSD_IN_DIM = SSD_D_INNER + SSD_CONV_DIM + SSD_N_HEADS
N_Z_TILES = SSD_D_INNER // LANES
N_X_TILES = SSD_D_INNER // LANES
N_CONV_TILES = SSD_CONV_DIM // LANES
N_IN_TILES = N_Z_TILES + N_CONV_TILES + 1
B_TILE0 = N_X_TILES
C_TILE0 = N_X_TILES + SSD_N_GROUPS
HALO_ROWS = 8
N_STAGE = 4
P_X0 = N_Z_TILES
P_C0 = N_Z_TILES + N_X_TILES
N_P_TILES = N_Z_TILES + N_X_TILES + SSD_N_GROUPS

ATTN_N_Q_HEADS = 16
ATTN_N_KV_HEADS = 4
ATTN_REP = 4
ATTN_WINDOW = 128
ATTN_Q_DIM = ATTN_N_Q_HEADS * HEAD_DIM
ATTN_KV_DIM = ATTN_N_KV_HEADS * HEAD_DIM
ATTN_QKV_DIM = ATTN_Q_DIM + 2 * ATTN_KV_DIM

D_FF = 4096
FF_CHUNK = 1024

ROW_TILE = 512
VMEM_LIMIT = 56 * 1024 * 1024


def _rms(x, gain):
    ms = jnp.mean(x * x, axis=-1, keepdims=True)
    return x * lax.rsqrt(ms + NORM_EPS) * gain


def _sigmoid(x):
    return 1.0 / (1.0 + jnp.exp2(x * (-LOG2_E)))


def _resident(shape):
    nd = len(shape)
    return pl.BlockSpec(shape, lambda *_: (0,) * nd, pipeline_mode=pl.Buffered(1))


def _params(*sem):
    return pltpu.CompilerParams(dimension_semantics=sem, vmem_limit_bytes=VMEM_LIMIT)


def _ssd_in_proj_kernel(x_ref, g_ref, w_ref, cw_ref, cb_ref, dtb_ref, alog_ref,
                        p_ref, bt_ref, cum_ref, cumt_ref, dtt_ref, wendt_ref, halo_ref, stage_ref):
    R = ROW_TILE
    T = SSD_CHUNK

    @pl.when(pl.program_id(0) % (SEQ // ROW_TILE) == 0)
    def _():
        halo_ref[...] = jnp.zeros_like(halo_ref)

    xn = _rms(x_ref[...], g_ref[...]).astype(BF16)
    pair = 2 * LANES

    dt_in = jnp.dot(xn, w_ref[:, (N_IN_TILES - 1) * LANES:], preferred_element_type=F32) + dtb_ref[...]
    dt = jnp.maximum(dt_in, 0.0) + jnp.log1p(jnp.exp(-jnp.abs(dt_in)))
    aq = dt * (-jnp.exp(alog_ref[...]))
    tri = (lax.broadcasted_iota(jnp.int32, (T, T), 0)
           >= lax.broadcasted_iota(jnp.int32, (T, T), 1)).astype(BF16)
    aq_hi = aq.astype(BF16)
    rem = aq - aq_hi.astype(F32)
    aq_mid = rem.astype(BF16)
    aq_lo = (rem - aq_mid.astype(F32)).astype(BF16)
    for c in range(R // T):
        rows = slice(c * T, (c + 1) * T)
        cum = (jnp.dot(tri, aq_hi[rows], preferred_element_type=F32)
               + jnp.dot(tri, aq_mid[rows], preferred_element_type=F32)
               + jnp.dot(tri, aq_lo[rows], preferred_element_type=F32))
        cum_last = cum[T - 1:T, :]
        cum2 = cum * LOG2_E
        cum_ref[rows, :] = cum2
        cumt_ref[:, rows] = cum2.T
        dtt_ref[:, rows] = dt[rows].T
        wendt_ref[:, rows] = (dt[rows] * jnp.exp(cum_last - cum)).T

    def z_pair(t):
        acc = jnp.dot(xn, w_ref[:, t * LANES:t * LANES + pair], preferred_element_type=F32)
        p_ref[t] = acc[:, :LANES]
        p_ref[t + 1] = acc[:, LANES:]

    def conv_silu(raw, kc):
        w = cw_ref[kc]
        slot = kc % N_STAGE
        stage_ref[slot, 0:HALO_ROWS, :] = halo_ref[kc]
        stage_ref[slot, HALO_ROWS:, :] = raw
        acc = cb_ref[kc] + w[SSD_CONV_WIDTH - 1:SSD_CONV_WIDTH] * raw
        for d in range(1, SSD_CONV_WIDTH):
            shifted = stage_ref[slot, HALO_ROWS - d:HALO_ROWS - d + R, :]
            acc = acc + w[SSD_CONV_WIDTH - 1 - d:SSD_CONV_WIDTH - d] * shifted
        halo_ref[kc] = raw[R - HALO_ROWS:]
        return acc * _sigmoid(acc)

    def put_conv(kc, val):
        if kc < B_TILE0:
            p_ref[P_X0 + kc] = val
        elif kc < C_TILE0:
            for c in range(R // T):
                bt_ref[kc - B_TILE0, :, c * T:(c + 1) * T] = val[c * T:(c + 1) * T].T
        else:
            p_ref[P_C0 + kc - C_TILE0] = val

    def conv_pair(t):
        col = (N_Z_TILES + t) * LANES
        acc = jnp.dot(xn, w_ref[:, col:col + pair], preferred_element_type=F32)
        put_conv(t, conv_silu(acc[:, :LANES], t))
        put_conv(t + 1, conv_silu(acc[:, LANES:], t + 1))

    for i in range(N_Z_TILES // 2):
        conv_pair(4 * i)
        z_pair(2 * i)
        conv_pair(4 * i + 2)


def _ssd_in_proj(x2d, gain, w_pad, conv_w, conv_b, dt_bias, a_log):
    m = x2d.shape[0]
    t_major = pl.BlockSpec((ROW_TILE, LANES), lambda i: (i, 0))
    h_major = pl.BlockSpec((LANES, ROW_TILE), lambda i: (0, i))
    return pl.pallas_call(
        _ssd_in_proj_kernel,
        grid=(m // ROW_TILE,),
        in_specs=[
            pl.BlockSpec((ROW_TILE, D_MODEL), lambda i: (i, 0)),
            _resident((1, D_MODEL)),
            _resident((D_MODEL, N_IN_TILES * LANES)),
            _resident((N_CONV_TILES, SSD_CONV_WIDTH, LANES)),
            _resident((N_CONV_TILES, 1, LANES)),
            _resident((1, LANES)),
            _resident((1, LANES)),
        ],
        out_specs=[
            pl.BlockSpec((N_P_TILES, ROW_TILE, LANES), lambda i: (0, i, 0)),
            pl.BlockSpec((SSD_N_GROUPS, SSD_D_STATE, ROW_TILE), lambda i: (0, 0, i)),
            t_major, h_major, h_major, h_major,
        ],
        out_shape=[
            jax.ShapeDtypeStruct((N_P_TILES, m, LANES), F32),
            jax.ShapeDtypeStruct((SSD_N_GROUPS, SSD_D_STATE, m), F32),
            jax.ShapeDtypeStruct((m, LANES), F32),
            jax.ShapeDtypeStruct((LANES, m), F32),
            jax.ShapeDtypeStruct((LANES, m), F32),
            jax.ShapeDtypeStruct((LANES, m), F32),
        ],
        scratch_shapes=[
            pltpu.VMEM((N_CONV_TILES, HALO_ROWS, LANES), F32),
            pltpu.VMEM((N_STAGE, HALO_ROWS + ROW_TILE, LANES), F32),
        ],
        compiler_params=_params("arbitrary"),
        name="ssd_in_proj",
    )(x2d, gain, w_pad, conv_w, conv_b, dt_bias, a_log)


def _ssd_kernel(p_ref, bt_ref, cum_ref, cumt_ref, dtt_ref, wendt_ref, dskip_ref, nw_ref, o_ref, state_ref):
    T = SSD_CHUNK
    lane = lax.broadcasted_iota(jnp.int32, (T, LANES), 1)
    row = lax.broadcasted_iota(jnp.int32, (T, LANES), 0)
    lane_lo = lane < HEAD_DIM
    causal = row >= lane

    @pl.when(pl.program_id(1) == 0)
    def _():
        state_ref[...] = jnp.zeros_like(state_ref)

    dec = jnp.exp2(cum_ref[T - 1:T, :])

    def group(g):
        cm = p_ref[P_C0 + g]
        bt = bt_ref[g]
        cb = jnp.dot(cm.astype(BF16), bt.astype(BF16), preferred_element_type=F32)
        gated = []
        for half in range(2):
            kx = 2 * g + half
            x = p_ref[P_X0 + kx]
            st = state_ref[kx]
            x_lo = jnp.where(lane_lo, x, 0.0).astype(BF16)
            x_hi = jnp.where(lane_lo, 0.0, x).astype(BF16)
            st_lo = jnp.where(lane_lo, st, 0.0).astype(BF16)
            st_hi = jnp.where(lane_lo, 0.0, st).astype(BF16)
            lhs_y = []
            lhs_s = []
            for i in range(2):
                h = 2 * kx + i
                crow = cumt_ref[h:h + 1, :]
                cbc = jnp.broadcast_to(cum_ref[:, h:h + 1], (T, LANES))
                decay = jnp.where(causal, jnp.exp2(cbc - crow), 0.0)
                lhs_y.append((cb * decay * dtt_ref[h:h + 1, :]).astype(BF16))
                lhs_y.append((cm * jnp.exp2(cbc)).astype(BF16))
                lhs_s.append((bt * wendt_ref[h:h + 1, :]).astype(BF16))
            y = jnp.dot(jnp.concatenate(lhs_y, axis=1),
                        jnp.concatenate([x_lo, st_lo, x_hi, st_hi], axis=0),
                        preferred_element_type=F32)
            y = y + dskip_ref[kx] * x
            d0 = jnp.broadcast_to(dec[:, 2 * kx:2 * kx + 1], (1, LANES))
            d1 = jnp.broadcast_to(dec[:, 2 * kx + 1:2 * kx + 2], (1, LANES))
            state_ref[kx] = st * jnp.where(lane_lo[0:1], d0, d1) + jnp.dot(
                jnp.concatenate(lhs_s, axis=1), jnp.concatenate([x_lo, x_hi], axis=0),
                preferred_element_type=F32)
            z = p_ref[kx]
            gated.append(y * (z * _sigmoid(z)))
        ss = (jnp.sum(gated[0] * gated[0], axis=-1, keepdims=True)
              + jnp.sum(gated[1] * gated[1], axis=-1, keepdims=True))
        r = lax.rsqrt(ss * (1.0 / (2 * LANES)) + NORM_EPS)
        for half in range(2):
            kx = 2 * g + half
            o_ref[kx] = (gated[half] * r * nw_ref[kx]).astype(o_ref.dtype)

    for g in range(SSD_N_GROUPS):
        group(g)


def _ssd_core(p, bt, cum, cumt, dtt, wendt, dskip, norm_w, batch):
    m = p.shape[1]
    n_chunks = m // batch // SSD_CHUNK
    T = SSD_CHUNK
    h_major = pl.BlockSpec((LANES, T), lambda b, c: (0, b * n_chunks + c))
    return pl.pallas_call(
        _ssd_kernel,
        grid=(batch, n_chunks),
        in_specs=[
            pl.BlockSpec((p.shape[0], T, LANES), lambda b, c: (0, b * n_chunks + c, 0)),
            pl.BlockSpec((SSD_N_GROUPS, SSD_D_STATE, T), lambda b, c: (0, 0, b * n_chunks + c)),
            pl.BlockSpec((T, LANES), lambda b, c: (b * n_chunks + c, 0)),
            h_major, h_major, h_major,
            _resident((N_X_TILES, 1, LANES)),
            _resident((N_X_TILES, 1, LANES)),
        ],
        out_specs=pl.BlockSpec((N_X_TILES, T, LANES), lambda b, c: (0, b * n_chunks + c, 0)),
        out_shape=jax.ShapeDtypeStruct((N_X_TILES, m, LANES), BF16),
        scratch_shapes=[pltpu.VMEM((N_X_TILES, SSD_D_STATE, LANES), F32)],
        compiler_params=_params("arbitrary", "arbitrary"),
        name="ssd_core",
    )(p, bt, cum, cumt, dtt, wendt, dskip, norm_w)


def _mlp_residual(h, g1_ref, wu_ref, wd_ref, g2_ref):
    u = _rms(h, g1_ref[...]).astype(BF16)
    acc = None
    for c in range(0, D_FF, FF_CHUNK):
        a = jnp.dot(u, wu_ref[:, c:c + FF_CHUNK], preferred_element_type=F32)
        a = jnp.square(jnp.maximum(a, 0.0)).astype(BF16)
        part = jnp.dot(a, wd_ref[c:c + FF_CHUNK, :], preferred_element_type=F32)
        acc = part if acc is None else acc + part
    return h + _rms(acc, g2_ref[...])


def _ssd_tail_kernel(y_ref, wo_ref, h_ref, gp_ref, g1_ref, wu_ref, wd_ref, g2_ref,
                     gq_ref, wq_ref, bq_ref, o_ref, qkv_ref):
    y = jnp.concatenate([y_ref[k] for k in range(N_X_TILES)], axis=1)
    mix = jnp.dot(y, wo_ref[...], preferred_element_type=F32)
    h = _mlp_residual(h_ref[...] + _rms(mix, gp_ref[...]), g1_ref, wu_ref, wd_ref, g2_ref)
    o_ref[...] = h
    xn = _rms(h, gq_ref[...]).astype(BF16)
    qkv = jnp.dot(xn, wq_ref[...], preferred_element_type=F32) + bq_ref[...]
    qkv_ref[...] = qkv.astype(qkv_ref.dtype)


def _attn_tail_kernel(a_ref, wo_ref, bo_ref, h_ref, gp_ref, g1_ref, wu_ref, wd_ref, g2_ref, o_ref):
    mix = jnp.dot(a_ref[...], wo_ref[...], preferred_element_type=F32) + bo_ref[...]
    o_ref[...] = _mlp_residual(h_ref[...] + _rms(mix, gp_ref[...]), g1_ref, wu_ref, wd_ref, g2_ref)


_ROWS = pl.BlockSpec((ROW_TILE, D_MODEL), lambda i: (i, 0))
_GAIN = (1, D_MODEL)


def _ssd_tail(y_tiles, w_out, h, g_post, g1, wu, wd, g2, g_qkv, w_qkv, b_qkv):
    m = h.shape[0]
    return pl.pallas_call(
        _ssd_tail_kernel,
        grid=(m // ROW_TILE,),
        in_specs=[
            pl.BlockSpec((N_X_TILES, ROW_TILE, LANES), lambda i: (0, i, 0)),
            _resident(w_out.shape), _ROWS, _resident(_GAIN),
            _resident(_GAIN), _resident(wu.shape), _resident(wd.shape), _resident(_GAIN),
            _resident(_GAIN), _resident(w_qkv.shape), _resident(b_qkv.shape),
        ],
        out_specs=[_ROWS, pl.BlockSpec((ROW_TILE, ATTN_QKV_DIM), lambda i: (i, 0))],
        out_shape=[jax.ShapeDtypeStruct((m, D_MODEL), F32),
                   jax.ShapeDtypeStruct((m, ATTN_QKV_DIM), BF16)],
        compiler_params=_params("arbitrary"),
        name="ssd_tail_mlp_qkv",
    )(y_tiles, w_out, h, g_post, g1, wu, wd, g2, g_qkv, w_qkv, b_qkv)


def _attn_tail(a, w_o, b_o, h, g_post, g1, wu, wd, g2):
    m = h.shape[0]
    return pl.pallas_call(
        _attn_tail_kernel,
        grid=(m // ROW_TILE,),
        in_specs=[
            pl.BlockSpec((ROW_TILE, ATTN_Q_DIM), lambda i: (i, 0)),
            _resident(w_o.shape), _resident(_GAIN), _ROWS, _resident(_GAIN),
            _resident(_GAIN), _resident(wu.shape), _resident(wd.shape), _resident(_GAIN),
        ],
        out_specs=_ROWS,
        out_shape=jax.ShapeDtypeStruct((m, D_MODEL), F32),
        compiler_params=_params("arbitrary"),
        name="attn_tail_mlp",
    )(a, w_o, b_o, h, g_post, g1, wu, wd, g2)


def _attn_kernel(sink_ref, q_ref, kp_ref, kc_ref, vp_ref, vc_ref, o_ref):
    W = ATTN_WINDOW
    kmin = jnp.where(pl.program_id(1) == 0, W, 0)
    kcat = jnp.concatenate([kp_ref[...], kc_ref[...]], axis=0)
    vcat = jnp.concatenate([vp_ref[...], vc_ref[...]], axis=0)
    qrow = lax.broadcasted_iota(jnp.int32, (2 * W, 2 * W), 0)
    kpos = lax.broadcasted_iota(jnp.int32, (2 * W, 2 * W), 1)
    qpos = jnp.where(qrow >= W, qrow - W, qrow) + W
    rel = qpos - kpos
    valid = (rel >= 0) & (rel < W) & (kpos >= kmin)
    lane = lax.broadcasted_iota(jnp.int32, (W, LANES), 1)
    lane_lo = lane < HEAD_DIM
    is_lo_row = lax.broadcasted_iota(jnp.int32, (2 * W, 1), 0) < W
    zero = jnp.zeros((), BF16)
    for t in range(ATTN_Q_DIM // LANES):
        pair = t // ATTN_REP
        qt = q_ref[:, t * LANES:(t + 1) * LANES] * jnp.asarray(HEAD_DIM ** -0.5, BF16)
        qs = jnp.concatenate([jnp.where(lane_lo, qt, zero), jnp.where(lane_lo, zero, qt)], axis=0)
        kt = kcat[:, pair * LANES:(pair + 1) * LANES]
        vt = vcat[:, pair * LANES:(pair + 1) * LANES]
        s = lax.dot_general(qs, kt, (((1,), (1,)), ((), ())), preferred_element_type=F32)
        s = jnp.where(valid, s, -jnp.inf)
        sink = jnp.where(is_lo_row, sink_ref[2 * t], sink_ref[2 * t + 1])
        mx = jnp.maximum(jnp.max(s, axis=-1, keepdims=True), sink)
        e = jnp.exp(s - mx)
        den = jnp.sum(e, axis=-1, keepdims=True) + jnp.exp(sink - mx)
        pv = jnp.dot(e.astype(BF16), vt, preferred_element_type=F32) * (1.0 / den)
        o_ref[:, t * LANES:(t + 1) * LANES] = jnp.where(lane_lo, pv[:W], pv[W:]).astype(o_ref.dtype)


def _attention(qkv, sinks, batch):
    m = qkv.shape[0]
    nb = m // batch // ATTN_WINDOW
    W = ATTN_WINDOW
    k_col = ATTN_Q_DIM // ATTN_KV_DIM
    v_col = k_col + 1

    def cur(b, i):
        return b * nb + i

    def prev(b, i):
        return jnp.maximum(b * nb + i - 1, 0)

    return pl.pallas_call(
        _attn_kernel,
        grid=(batch, nb),
        in_specs=[
            pl.BlockSpec(memory_space=pltpu.SMEM),
            pl.BlockSpec((W, ATTN_Q_DIM), lambda b, i: (cur(b, i), 0)),
            pl.BlockSpec((W, ATTN_KV_DIM), lambda b, i: (prev(b, i), k_col)),
            pl.BlockSpec((W, ATTN_KV_DIM), lambda b, i: (cur(b, i), k_col)),
            pl.BlockSpec((W, ATTN_KV_DIM), lambda b, i: (prev(b, i), v_col)),
            pl.BlockSpec((W, ATTN_KV_DIM), lambda b, i: (cur(b, i), v_col)),
        ],
        out_specs=pl.BlockSpec((W, ATTN_Q_DIM), lambda b, i: (cur(b, i), 0)),
        out_shape=jax.ShapeDtypeStruct((m, ATTN_Q_DIM), BF16),
        compiler_params=_params("arbitrary", "arbitrary"),
        name="swa_sink_attention",
    )(sinks, qkv, qkv, qkv, qkv, qkv)


def _permute_q_heads(v, axis):
    inner = v.shape[axis] // ATTN_N_Q_HEADS
    shape = v.shape[:axis] + (ATTN_N_KV_HEADS // 2, 2, ATTN_REP, inner) + v.shape[axis + 1:]
    order = list(range(len(shape)))
    order[axis + 1], order[axis + 2] = axis + 2, axis + 1
    return v.reshape(shape).transpose(order).reshape(v.shape)


def _lane_tiles(v, n_tiles):
    return v.astype(F32).reshape(n_tiles, 1, LANES)


def kernel(x, ssd_w_in, ssd_conv_w, ssd_conv_b, ssd_dt_bias, ssd_a_log, ssd_d, ssd_norm_w, ssd_w_out, attn_w_qkv, attn_b_qkv, attn_sinks, attn_w_o, attn_b_o, mlp_w_up, mlp_w_down, mix_pre_norm, mix_post_norm, ffn_pre_norm, ffn_post_norm):
    batch, seq, d = x.shape
    m = batch * seq
    h = x.reshape(m, d)
    row = lambda v: v.astype(F32).reshape(1, -1)

    w_in = jnp.pad(ssd_w_in[0], ((0, 0), (0, N_IN_TILES * LANES - SSD_IN_DIM))).astype(BF16)
    conv_w = ssd_conv_w[0].astype(F32).reshape(SSD_CONV_WIDTH, N_CONV_TILES, LANES).transpose(1, 0, 2)
    pad_heads = lambda v: jnp.pad(v.astype(F32), (0, LANES - SSD_N_HEADS)).reshape(1, LANES)
    assert seq == SEQ and seq % ROW_TILE == 0
    p, bt, cum, cumt, dtt, wendt = _ssd_in_proj(
        h, row(mix_pre_norm[0]), w_in, conv_w, _lane_tiles(ssd_conv_b[0], N_CONV_TILES),
        pad_heads(ssd_dt_bias[0]), pad_heads(ssd_a_log[0]))
    y = _ssd_core(
        p, bt, cum, cumt, dtt, wendt, _lane_tiles(jnp.repeat(ssd_d[0], HEAD_DIM), N_X_TILES),
        _lane_tiles(ssd_norm_w[0], N_X_TILES), batch)
    wq, wkv = attn_w_qkv[0][:, :ATTN_Q_DIM], attn_w_qkv[0][:, ATTN_Q_DIM:]
    bq, bkv = attn_b_qkv[0][:ATTN_Q_DIM], attn_b_qkv[0][ATTN_Q_DIM:]
    w_qkv = jnp.concatenate([_permute_q_heads(wq, 1), wkv], axis=1).astype(BF16)
    b_qkv = jnp.concatenate([_permute_q_heads(bq, 0), bkv]).astype(F32).reshape(1, -1)
    h, qkv = _ssd_tail(
        y, ssd_w_out[0].astype(BF16), h, row(mix_post_norm[0]),
        row(ffn_pre_norm[0]), mlp_w_up[0].astype(BF16), mlp_w_down[0].astype(BF16), row(ffn_post_norm[0]),
        row(mix_pre_norm[1]), w_qkv, b_qkv)

    a = _attention(qkv, _permute_q_heads(attn_sinks[0], 0).astype(F32), batch)
    h = _attn_tail(
        a, _permute_q_heads(attn_w_o[0], 0).astype(BF16), row(attn_b_o[0]), h, row(mix_post_norm[1]),
        row(ffn_pre_norm[1]), mlp_w_up[1].astype(BF16), mlp_w_down[1].astype(BF16), row(ffn_post_norm[1]))
    return h.reshape(batch, seq, d)
```

```python
import functools

import jax
import jax.numpy as jnp
from jax import lax
from jax.experimental import pallas as pl
from jax.experimental.pallas import tpu as pltpu

F32 = jnp.float32
BF16 = jnp.bfloat16

D_MODEL = 1024
SEQ = 4096
NORM_EPS = 1e-6
LOG2_E = 1.4426950408889634

LANES = 128
HEAD_DIM = 64

SSD_D_INNER = 2048
SSD_N_HEADS = 32
SSD_N_GROUPS = 8
SSD_D_STATE = 128
SSD_CONV_WIDTH = 4
SSD_CHUNK = 128
SSD_CONV_DIM = SSD_D_INNER + 2 * SSD_N_GROUPS * SSD_D_STATE
SSD_IN_DIM = SSD_D_INNER + SSD_CONV_DIM + SSD_N_HEADS
N_Z_TILES = SSD_D_INNER // LANES
N_X_TILES = SSD_D_INNER // LANES
N_CONV_TILES = SSD_CONV_DIM // LANES
N_IN_TILES = N_Z_TILES + N_CONV_TILES + 1
B_TILE0 = N_X_TILES
C_TILE0 = N_X_TILES + SSD_N_GROUPS
HALO_ROWS = 8
SSD_CHUNKS_PER_STEP = 2
N_STAGE = 4
P_X0 = N_Z_TILES
P_C0 = N_Z_TILES + N_X_TILES
N_P_TILES = N_Z_TILES + N_X_TILES + SSD_N_GROUPS

ATTN_N_Q_HEADS = 16
ATTN_N_KV_HEADS = 4
ATTN_REP = 4
ATTN_WINDOW = 128
ATTN_Q_DIM = ATTN_N_Q_HEADS * HEAD_DIM
ATTN_KV_DIM = ATTN_N_KV_HEADS * HEAD_DIM
ATTN_QKV_DIM = ATTN_Q_DIM + 2 * ATTN_KV_DIM

D_FF = 4096
FF_CHUNK = 1024

ROW_TILE = 512
VMEM_LIMIT = 56 * 1024 * 1024


def _rms(x, gain):
    ms = jnp.mean(x * x, axis=-1, keepdims=True)
    return x * lax.rsqrt(ms + NORM_EPS) * gain


def _sigmoid(x):
    return 1.0 / (1.0 + jnp.exp2(x * (-LOG2_E)))


def _resident(shape):
    nd = len(shape)
    return pl.BlockSpec(shape, lambda *_: (0,) * nd, pipeline_mode=pl.Buffered(1))


def _params(*sem):
    return pltpu.CompilerParams(dimension_semantics=sem, vmem_limit_bytes=VMEM_LIMIT)


def _ssd_in_proj_kernel(x_ref, g_ref, w_ref, cw_ref, cb_ref, dtb_ref, alog_ref,
                        p_ref, bt_ref, cum_ref, cumt_ref, dtt_ref, wendt_ref, halo_ref, stage_ref):
    R = ROW_TILE
    T = SSD_CHUNK

    @pl.when(pl.program_id(0) % (SEQ // ROW_TILE) == 0)
    def _():
        halo_ref[...] = jnp.zeros_like(halo_ref)

    xn = _rms(x_ref[...], g_ref[...]).astype(BF16)
    pair = 2 * LANES

    dt_in = jnp.dot(xn, w_ref[:, (N_IN_TILES - 1) * LANES:], preferred_element_type=F32) + dtb_ref[...]
    dt = jnp.maximum(dt_in, 0.0) + jnp.log1p(jnp.exp(-jnp.abs(dt_in)))
    aq = dt * (-jnp.exp(alog_ref[...]))
    tri = (lax.broadcasted_iota(jnp.int32, (T, T), 0)
           >= lax.broadcasted_iota(jnp.int32, (T, T), 1)).astype(BF16)
    aq_hi = aq.astype(BF16)
    rem = aq - aq_hi.astype(F32)
    aq_mid = rem.astype(BF16)
    aq_lo = (rem - aq_mid.astype(F32)).astype(BF16)
    for c in range(R // T):
        rows = slice(c * T, (c + 1) * T)
        cum = (jnp.dot(tri, aq_hi[rows], preferred_element_type=F32)
               + jnp.dot(tri, aq_mid[rows], preferred_element_type=F32)
               + jnp.dot(tri, aq_lo[rows], preferred_element_type=F32))
        cum_last = cum[T - 1:T, :]
        cum2 = cum * LOG2_E
        cum_ref[rows, :] = cum2
        cumt_ref[:, rows] = cum2.T
        dtt_ref[:, rows] = dt[rows].T
        wendt_ref[:, rows] = (dt[rows] * jnp.exp(cum_last - cum)).T

    def z_pair(t):
        acc = jnp.dot(xn, w_ref[:, t * LANES:t * LANES + pair], preferred_element_type=F32)
        gate = acc * _sigmoid(acc)
        p_ref[t] = gate[:, :LANES]
        p_ref[t + 1] = gate[:, LANES:]

    def conv_silu(raw, kc):
        w = cw_ref[kc]
        slot = kc % N_STAGE
        stage_ref[slot, 0:HALO_ROWS, :] = halo_ref[kc]
        stage_ref[slot, HALO_ROWS:, :] = raw
        acc = cb_ref[kc] + w[SSD_CONV_WIDTH - 1:SSD_CONV_WIDTH] * raw
        for d in range(1, SSD_CONV_WIDTH):
            shifted = stage_ref[slot, HALO_ROWS - d:HALO_ROWS - d + R, :]
            acc = acc + w[SSD_CONV_WIDTH - 1 - d:SSD_CONV_WIDTH - d] * shifted
        halo_ref[kc] = raw[R - HALO_ROWS:]
        return acc * _sigmoid(acc)

    def put_conv(kc, val):
        if kc < B_TILE0:
            p_ref[P_X0 + kc] = val
        elif kc < C_TILE0:
            for c in range(R // T):
                bt_ref[kc - B_TILE0, :, c * T:(c + 1) * T] = val[c * T:(c + 1) * T].T
        else:
            p_ref[P_C0 + kc - C_TILE0] = val

    def conv_pair(t):
        col = (N_Z_TILES + t) * LANES
        acc = jnp.dot(xn, w_ref[:, col:col + pair], preferred_element_type=F32)
        put_conv(t, conv_silu(acc[:, :LANES], t))
        put_conv(t + 1, conv_silu(acc[:, LANES:], t + 1))

    for i in range(N_Z_TILES // 2):
        conv_pair(4 * i)
        z_pair(2 * i)
        conv_pair(4 * i + 2)


def _ssd_in_proj(x2d, gain, w_pad, conv_w, conv_b, dt_bias, a_log):
    m = x2d.shape[0]
    t_major = pl.BlockSpec((ROW_TILE, LANES), lambda i: (i, 0))
    h_major = pl.BlockSpec((LANES, ROW_TILE), lambda i: (0, i))
    return pl.pallas_call(
        _ssd_in_proj_kernel,
        grid=(m // ROW_TILE,),
        in_specs=[
            pl.BlockSpec((ROW_TILE, D_MODEL), lambda i: (i, 0)),
            _resident((1, D_MODEL)),
            _resident((D_MODEL, N_IN_TILES * LANES)),
            _resident((N_CONV_TILES, SSD_CONV_WIDTH, LANES)),
            _resident((N_CONV_TILES, 1, LANES)),
            _resident((1, LANES)),
            _resident((1, LANES)),
        ],
        out_specs=[
            pl.BlockSpec((N_P_TILES, ROW_TILE, LANES), lambda i: (0, i, 0)),
            pl.BlockSpec((SSD_N_GROUPS, SSD_D_STATE, ROW_TILE), lambda i: (0, 0, i)),
            t_major, h_major, h_major, h_major,
        ],
        out_shape=[
            jax.ShapeDtypeStruct((N_P_TILES, m, LANES), F32),
            jax.ShapeDtypeStruct((SSD_N_GROUPS, SSD_D_STATE, m), F32),
            jax.ShapeDtypeStruct((m, LANES), F32),
            jax.ShapeDtypeStruct((LANES, m), F32),
            jax.ShapeDtypeStruct((LANES, m), F32),
            jax.ShapeDtypeStruct((LANES, m), F32),
        ],
        scratch_shapes=[
            pltpu.VMEM((N_CONV_TILES, HALO_ROWS, LANES), F32),
            pltpu.VMEM((N_STAGE, HALO_ROWS + ROW_TILE, LANES), F32),
        ],
        compiler_params=_params("arbitrary"),
        name="ssd_in_proj",
    )(x2d, gain, w_pad, conv_w, conv_b, dt_bias, a_log)


def _ssd_kernel(p_ref, bt_ref, cum_ref, cumt_ref, dtt_ref, wendt_ref, dskip_ref, nw_ref, o_ref, state_ref):
    T = SSD_CHUNK
    lane = lax.broadcasted_iota(jnp.int32, (T, LANES), 1)
    row = lax.broadcasted_iota(jnp.int32, (T, LANES), 0)
    lane_lo = lane < HEAD_DIM
    causal = row >= lane

    @pl.when(pl.program_id(1) == 0)
    def _():
        state_ref[...] = jnp.zeros_like(state_ref)

    def group(rows, dec, g):
        cm = p_ref[P_C0 + g, rows, :].astype(BF16)
        bt = bt_ref[g, :, rows].astype(BF16)
        cb = jnp.dot(cm, bt, preferred_element_type=F32).astype(BF16)
        gated = []
        for half in range(2):
            kx = 2 * g + half
            x = p_ref[P_X0 + kx, rows, :]
            st = state_ref[kx]
            x_lo = jnp.where(lane_lo, x, 0.0).astype(BF16)
            x_hi = jnp.where(lane_lo, 0.0, x).astype(BF16)
            st_lo = jnp.where(lane_lo, st, 0.0).astype(BF16)
            st_hi = jnp.where(lane_lo, 0.0, st).astype(BF16)
            lhs_y = []
            lhs_s = []
            for i in range(2):
                h = 2 * kx + i
                crow = cumt_ref[h:h + 1, rows]
                cbc = jnp.broadcast_to(cum_ref[rows, h:h + 1], (T, LANES))
                decay = jnp.where(causal, jnp.exp2(cbc - crow), 0.0).astype(BF16)
                lhs_y.append(cb * decay * dtt_ref[h:h + 1, rows].astype(BF16))
                lhs_y.append(cm * jnp.exp2(cbc).astype(BF16))
                lhs_s.append(bt * wendt_ref[h:h + 1, rows].astype(BF16))
            y = jnp.dot(jnp.concatenate(lhs_y, axis=1),
                        jnp.concatenate([x_lo, st_lo, x_hi, st_hi], axis=0),
                        preferred_element_type=F32)
            y = y + dskip_ref[kx] * x
            d0 = jnp.broadcast_to(dec[:, 2 * kx:2 * kx + 1], (1, LANES))
            d1 = jnp.broadcast_to(dec[:, 2 * kx + 1:2 * kx + 2], (1, LANES))
            state_ref[kx] = st * jnp.where(lane_lo[0:1], d0, d1) + jnp.dot(
                jnp.concatenate(lhs_s, axis=1), jnp.concatenate([x_lo, x_hi], axis=0),
                preferred_element_type=F32)
            gated.append(y * p_ref[kx, rows, :])
        ss = (jnp.sum(gated[0] * gated[0], axis=-1, keepdims=True)
              + jnp.sum(gated[1] * gated[1], axis=-1, keepdims=True))
        r = lax.rsqrt(ss * (1.0 / (2 * LANES)) + NORM_EPS)
        for half in range(2):
            kx = 2 * g + half
            o_ref[kx, rows, :] = (gated[half] * r * nw_ref[kx]).astype(o_ref.dtype)

    for ci in range(SSD_CHUNKS_PER_STEP):
        rows = slice(ci * T, (ci + 1) * T)
        dec = jnp.exp2(cum_ref[(ci + 1) * T - 1:(ci + 1) * T, :])
        for g in range(SSD_N_GROUPS):
            group(rows, dec, g)


def _ssd_core(p, bt, cum, cumt, dtt, wendt, dskip, norm_w, batch):
    m = p.shape[1]
    R = SSD_CHUNKS_PER_STEP * SSD_CHUNK
    n_steps = m // batch // R
    h_major = pl.BlockSpec((LANES, R), lambda b, c: (0, b * n_steps + c))
    return pl.pallas_call(
        _ssd_kernel,
        grid=(batch, n_steps),
        in_specs=[
            pl.BlockSpec((p.shape[0], R, LANES), lambda b, c: (0, b * n_steps + c, 0)),
            pl.BlockSpec((SSD_N_GROUPS, SSD_D_STATE, R), lambda b, c: (0, 0, b * n_steps + c)),
            pl.BlockSpec((R, LANES), lambda b, c: (b * n_steps + c, 0)),
            h_major, h_major, h_major,
            _resident((N_X_TILES, 1, LANES)),
            _resident((N_X_TILES, 1, LANES)),
        ],
        out_specs=pl.BlockSpec((N_X_TILES, R, LANES), lambda b, c: (0, b * n_steps + c, 0)),
        out_shape=jax.ShapeDtypeStruct((N_X_TILES, m, LANES), BF16),
        scratch_shapes=[pltpu.VMEM((N_X_TILES, SSD_D_STATE, LANES), F32)],
        compiler_params=_params("arbitrary", "arbitrary"),
        name="ssd_core",
    )(p, bt, cum, cumt, dtt, wendt, dskip, norm_w)


def _mlp_residual(h, g1_ref, wu_ref, wd_ref, g2_ref):
    u = _rms(h, g1_ref[...]).astype(BF16)
    acc = None
    for c in range(0, D_FF, FF_CHUNK):
        a = jnp.dot(u, wu_ref[:, c:c + FF_CHUNK], preferred_element_type=F32)
        a = jnp.square(jnp.maximum(a, 0.0)).astype(BF16)
        part = jnp.dot(a, wd_ref[c:c + FF_CHUNK, :], preferred_element_type=F32)
        acc = part if acc is None else acc + part
    return h + _rms(acc, g2_ref[...])


def _ssd_tail_kernel(y_ref, wo_ref, h_ref, gp_ref, g1_ref, wu_ref, wd_ref, g2_ref,
                     gq_ref, wq_ref, bq_ref, o_ref, qkv_ref):
    y = jnp.concatenate([y_ref[k] for k in range(N_X_TILES)], axis=1)
    mix = jnp.dot(y, wo_ref[...], preferred_element_type=F32)
    h = _mlp_residual(h_ref[...] + _rms(mix, gp_ref[...]), g1_ref, wu_ref, wd_ref, g2_ref)
    o_ref[...] = h
    xn = _rms(h, gq_ref[...]).astype(BF16)
    qkv = jnp.dot(xn, wq_ref[...], preferred_element_type=F32) + bq_ref[...]
    qkv_ref[...] = qkv.astype(qkv_ref.dtype)


def _attn_tail_kernel(a_ref, wo_ref, bo_ref, h_ref, gp_ref, g1_ref, wu_ref, wd_ref, g2_ref, o_ref):
    mix = jnp.dot(a_ref[...], wo_ref[...], preferred_element_type=F32) + bo_ref[...]
    o_ref[...] = _mlp_residual(h_ref[...] + _rms(mix, gp_ref[...]), g1_ref, wu_ref, wd_ref, g2_ref)


_ROWS = pl.BlockSpec((ROW_TILE, D_MODEL), lambda i: (i, 0))
_GAIN = (1, D_MODEL)


def _ssd_tail(y_tiles, w_out, h, g_post, g1, wu, wd, g2, g_qkv, w_qkv, b_qkv):
    m = h.shape[0]
    return pl.pallas_call(
        _ssd_tail_kernel,
        grid=(m // ROW_TILE,),
        in_specs=[
            pl.BlockSpec((N_X_TILES, ROW_TILE, LANES), lambda i: (0, i, 0)),
            _resident(w_out.shape), _ROWS, _resident(_GAIN),
            _resident(_GAIN), _resident(wu.shape), _resident(wd.shape), _resident(_GAIN),
            _resident(_GAIN), _resident(w_qkv.shape), _resident(b_qkv.shape),
        ],
        out_specs=[_ROWS, pl.BlockSpec((ROW_TILE, ATTN_QKV_DIM), lambda i: (i, 0))],
        out_shape=[jax.ShapeDtypeStruct((m, D_MODEL), F32),
                   jax.ShapeDtypeStruct((m, ATTN_QKV_DIM), BF16)],
        compiler_params=_params("arbitrary"),
        name="ssd_tail_mlp_qkv",
    )(y_tiles, w_out, h, g_post, g1, wu, wd, g2, g_qkv, w_qkv, b_qkv)


def _attn_tail(a, w_o, b_o, h, g_post, g1, wu, wd, g2):
    m = h.shape[0]
    return pl.pallas_call(
        _attn_tail_kernel,
        grid=(m // ROW_TILE,),
        in_specs=[
            pl.BlockSpec((ROW_TILE, ATTN_Q_DIM), lambda i: (i, 0)),
            _resident(w_o.shape), _resident(_GAIN), _ROWS, _resident(_GAIN),
            _resident(_GAIN), _resident(wu.shape), _resident(wd.shape), _resident(_GAIN),
        ],
        out_specs=_ROWS,
        out_shape=jax.ShapeDtypeStruct((m, D_MODEL), F32),
        compiler_params=_params("arbitrary"),
        name="attn_tail_mlp",
    )(a, w_o, b_o, h, g_post, g1, wu, wd, g2)


def _attn_kernel(sink_ref, q_ref, kp_ref, kc_ref, vp_ref, vc_ref, o_ref):
    W = ATTN_WINDOW
    PACK = 16
    kmin = jnp.where(pl.program_id(1) == 0, W, 0)
    top = lax.broadcasted_iota(jnp.int32, (PACK, 2 * LANES), 0) == 0
    zero = jnp.zeros((), BF16)

    def without_row0(prev_ref, cur_ref):
        prev = prev_ref[...]
        return jnp.concatenate([jnp.where(top, zero, prev[:PACK]), prev[PACK:], cur_ref[...]], axis=0)

    kcat = without_row0(kp_ref, kc_ref)
    vcat = without_row0(vp_ref, vc_ref)
    qrow = lax.broadcasted_iota(jnp.int32, (2 * W, 2 * W), 0)
    kpos = lax.broadcasted_iota(jnp.int32, (2 * W, 2 * W), 1)
    qpos = jnp.where(qrow >= W, qrow - W, qrow) + W
    rel = qpos - kpos
    valid = ((rel >= 0) & (rel < W) & (kpos >= kmin)) | (kpos == 0)
    bias = jnp.where(valid, 0.0, -jnp.inf)
    lane = lax.broadcasted_iota(jnp.int32, (W, LANES), 1)
    lane_lo = lane < HEAD_DIM
    q_extra = jnp.concatenate([jnp.where(lane < 3, 1.0, 0.0),
                               jnp.where((lane >= 3) & (lane < 6), 1.0, 0.0)], axis=0).astype(BF16)
    lane1 = lax.broadcasted_iota(jnp.int32, (1, LANES), 1)
    is_piece0 = (lane1 == 0) | (lane1 == 3)
    is_piece1 = (lane1 == 1) | (lane1 == 4)
    for t in range(ATTN_Q_DIM // LANES):
        pair = t // ATTN_REP
        qt = q_ref[:, t * LANES:(t + 1) * LANES] * jnp.asarray(HEAD_DIM ** -0.5, BF16)
        qs = jnp.concatenate([jnp.where(lane_lo, qt, zero), jnp.where(lane_lo, zero, qt)], axis=0)
        sv = sink_ref[t]
        s_hi = sv.astype(BF16).astype(F32)
        s_mid = (sv - s_hi).astype(BF16).astype(F32)
        s_lo = (sv - s_hi - s_mid).astype(BF16).astype(F32)
        srow = jnp.where(is_piece0, s_hi, jnp.where(is_piece1, s_mid, s_lo))
        k_extra = jnp.concatenate(
            [jnp.where(top[:, :LANES], jnp.broadcast_to(srow, (PACK, LANES)), 0.0).astype(BF16),
             jnp.zeros((2 * W - PACK, LANES), BF16)], axis=0)
        kt = jnp.concatenate([kcat[:, pair * LANES:(pair + 1) * LANES], k_extra], axis=1)
        vt = vcat[:, pair * LANES:(pair + 1) * LANES]
        s = lax.dot_general(jnp.concatenate([qs, q_extra], axis=1), kt, (((1,), (1,)), ((), ())),
                            preferred_element_type=F32) + bias
        mx = jnp.max(s, axis=-1, keepdims=True)
        e = jnp.exp(s - mx)
        den = jnp.sum(e, axis=-1, keepdims=True)
        pv = jnp.dot(e.astype(BF16), vt, preferred_element_type=F32) * (1.0 / den)
        o_ref[:, t * LANES:(t + 1) * LANES] = jnp.where(lane_lo, pv[:W], pv[W:]).astype(o_ref.dtype)


def _attention(qkv, sinks, batch):
    m = qkv.shape[0]
    nb = m // batch // ATTN_WINDOW
    W = ATTN_WINDOW
    k_col = ATTN_Q_DIM // ATTN_KV_DIM
    v_col = k_col + 1

    def cur(b, i):
        return b * nb + i

    def prev(b, i):
        return jnp.maximum(b * nb + i - 1, 0)

    return pl.pallas_call(
        _attn_kernel,
        grid=(batch, nb),
        in_specs=[
            _resident(sinks.shape),
            pl.BlockSpec((W, ATTN_Q_DIM), lambda b, i: (cur(b, i), 0)),
            pl.BlockSpec((W, ATTN_KV_DIM), lambda b, i: (prev(b, i), k_col)),
            pl.BlockSpec((W, ATTN_KV_DIM), lambda b, i: (cur(b, i), k_col)),
            pl.BlockSpec((W, ATTN_KV_DIM), lambda b, i: (prev(b, i), v_col)),
            pl.BlockSpec((W, ATTN_KV_DIM), lambda b, i: (cur(b, i), v_col)),
        ],
        out_specs=pl.BlockSpec((W, ATTN_Q_DIM), lambda b, i: (cur(b, i), 0)),
        out_shape=jax.ShapeDtypeStruct((m, ATTN_Q_DIM), BF16),
        compiler_params=_params("arbitrary", "arbitrary"),
        name="swa_sink_attention",
    )(sinks, qkv, qkv, qkv, qkv, qkv)


def _permute_q_heads(v, axis):
    inner = v.shape[axis] // ATTN_N_Q_HEADS
    shape = v.shape[:axis] + (ATTN_N_KV_HEADS // 2, 2, ATTN_REP, inner) + v.shape[axis + 1:]
    order = list(range(len(shape)))
    order[axis + 1], order[axis + 2] = axis + 2, axis + 1
    return v.reshape(shape).transpose(order).reshape(v.shape)


def _lane_tiles(v, n_tiles):
    return v.astype(F32).reshape(n_tiles, 1, LANES)


def kernel(x, ssd_w_in, ssd_conv_w, ssd_conv_b, ssd_dt_bias, ssd_a_log, ssd_d, ssd_norm_w, ssd_w_out, attn_w_qkv, attn_b_qkv, attn_sinks, attn_w_o, attn_b_o, mlp_w_up, mlp_w_down, mix_pre_norm, mix_post_norm, ffn_pre_norm, ffn_post_norm):
    batch, seq, d = x.shape
    m = batch * seq
    h = x.reshape(m, d)
    row = lambda v: v.astype(F32).reshape(1, -1)

    w_in = jnp.pad(ssd_w_in[0], ((0, 0), (0, N_IN_TILES * LANES - SSD_IN_DIM))).astype(BF16)
    conv_w = ssd_conv_w[0].astype(F32).reshape(SSD_CONV_WIDTH, N_CONV_TILES, LANES).transpose(1, 0, 2)
    pad_heads = lambda v: jnp.pad(v.astype(F32), (0, LANES - SSD_N_HEADS)).reshape(1, LANES)
    assert seq == SEQ and seq % ROW_TILE == 0
    p, bt, cum, cumt, dtt, wendt = _ssd_in_proj(
        h, row(mix_pre_norm[0]), w_in, conv_w, _lane_tiles(ssd_conv_b[0], N_CONV_TILES),
        pad_heads(ssd_dt_bias[0]), pad_heads(ssd_a_log[0]))
    y = _ssd_core(
        p, bt, cum, cumt, dtt, wendt, _lane_tiles(jnp.repeat(ssd_d[0], HEAD_DIM), N_X_TILES),
        _lane_tiles(ssd_norm_w[0], N_X_TILES), batch)
    wq, wkv = attn_w_qkv[0][:, :ATTN_Q_DIM], attn_w_qkv[0][:, ATTN_Q_DIM:]
    bq, bkv = attn_b_qkv[0][:ATTN_Q_DIM], attn_b_qkv[0][ATTN_Q_DIM:]
    w_qkv = jnp.concatenate([_permute_q_heads(wq, 1), wkv], axis=1).astype(BF16)
    b_qkv = jnp.concatenate([_permute_q_heads(bq, 0), bkv]).astype(F32).reshape(1, -1)
    h, qkv = _ssd_tail(
        y, ssd_w_out[0].astype(BF16), h, row(mix_post_norm[0]),
        row(ffn_pre_norm[0]), mlp_w_up[0].astype(BF16), mlp_w_down[0].astype(BF16), row(ffn_post_norm[0]),
        row(mix_pre_norm[1]), w_qkv, b_qkv)

    sink_pairs = _permute_q_heads(attn_sinks[0], 0).astype(F32).reshape(ATTN_Q_DIM // LANES, 2, 1)
    sink_rows = jnp.pad(jnp.broadcast_to(sink_pairs, sink_pairs.shape[:2] + (3,)).reshape(-1, 1, 6),
                        ((0, 0), (0, 0), (0, LANES - 6)))
    a = _attention(qkv, sink_rows, batch)
    h = _attn_tail(
        a, _permute_q_heads(attn_w_o[0], 0).astype(BF16), row(attn_b_o[0]), h, row(mix_post_norm[1]),
        row(ffn_pre_norm[1]), mlp_w_up[1].astype(BF16), mlp_w_down[1].astype(BF16), row(ffn_post_norm[1]))
    return h.reshape(batch, seq, d)
```

```python
import functools

import jax
import jax.numpy as jnp
from jax import lax
from jax.experimental import pallas as pl
from jax.experimental.pallas import tpu as pltpu

F32 = jnp.float32
BF16 = jnp.bfloat16

D_MODEL = 1024
SEQ = 4096
NORM_EPS = 1e-6
LOG2_E = 1.4426950408889634

LANES = 128
HEAD_DIM = 64

SSD_D_INNER = 2048
SSD_N_HEADS = 32
SSD_N_GROUPS = 8
SSD_D_STATE = 128
SSD_CONV_WIDTH = 4
SSD_CHUNK = 128
SSD_CONV_DIM = SSD_D_INNER + 2 * SSD_N_GROUPS * SSD_D_STATE
SSD_IN_DIM = SSD_D_INNER + SSD_CONV_DIM + SSD_N_HEADS
N_Z_TILES = SSD_D_INNER // LANES
N_X_TILES = SSD_D_INNER // LANES
N_CONV_TILES = SSD_CONV_DIM // LANES
N_IN_TILES = N_Z_TILES + N_CONV_TILES + 1
B_TILE0 = N_X_TILES
C_TILE0 = N_X_TILES + SSD_N_GROUPS
HALO_ROWS = 8
SSD_CHUNKS_PER_STEP = 4
N_STAGE = 4
P_X0 = N_Z_TILES
N_P_TILES = N_Z_TILES + N_X_TILES

ATTN_N_Q_HEADS = 16
ATTN_N_KV_HEADS = 4
ATTN_REP = 4
ATTN_WINDOW = 128
ATTN_Q_DIM = ATTN_N_Q_HEADS * HEAD_DIM
ATTN_KV_DIM = ATTN_N_KV_HEADS * HEAD_DIM
ATTN_QKV_DIM = ATTN_Q_DIM + 2 * ATTN_KV_DIM

D_FF = 4096
FF_CHUNK = 1024

ROW_TILE = 512
VMEM_LIMIT = 56 * 1024 * 1024


def _rms(x, gain):
    ms = jnp.mean(x * x, axis=-1, keepdims=True)
    return x * lax.rsqrt(ms + NORM_EPS) * gain


def _sigmoid(x):
    return 1.0 / (1.0 + jnp.exp2(x * (-LOG2_E)))


def _resident(shape):
    nd = len(shape)
    return pl.BlockSpec(shape, lambda *_: (0,) * nd, pipeline_mode=pl.Buffered(1))


def _params(*sem):
    return pltpu.CompilerParams(dimension_semantics=sem, vmem_limit_bytes=VMEM_LIMIT)


def _ssd_in_proj_kernel(x_ref, g_ref, w_ref, cw_ref, cb_ref, dtb_ref, alog_ref,
                        p_ref, c_ref, bt_ref, cum_ref, cumt_ref, dtt_ref, wendt_ref, halo_ref, stage_ref):
    R = ROW_TILE
    T = SSD_CHUNK

    @pl.when(pl.program_id(0) % (SEQ // ROW_TILE) == 0)
    def _():
        halo_ref[...] = jnp.zeros_like(halo_ref)

    xn = _rms(x_ref[...], g_ref[...]).astype(BF16)
    pair = 2 * LANES

    dt_in = jnp.dot(xn, w_ref[:, (N_IN_TILES - 1) * LANES:], preferred_element_type=F32) + dtb_ref[...]
    dt = jnp.maximum(dt_in, 0.0) + jnp.log1p(jnp.exp(-jnp.abs(dt_in)))
    aq = dt * (-jnp.exp(alog_ref[...]))
    tri = (lax.broadcasted_iota(jnp.int32, (T, T), 0)
           >= lax.broadcasted_iota(jnp.int32, (T, T), 1)).astype(BF16)
    aq_hi = aq.astype(BF16)
    rem = aq - aq_hi.astype(F32)
    aq_mid = rem.astype(BF16)
    aq_lo = (rem - aq_mid.astype(F32)).astype(BF16)
    for c in range(R // T):
        rows = slice(c * T, (c + 1) * T)
        cum = (jnp.dot(tri, aq_hi[rows], preferred_element_type=F32)
               + jnp.dot(tri, aq_mid[rows], preferred_element_type=F32)
               + jnp.dot(tri, aq_lo[rows], preferred_element_type=F32))
        cum_last = cum[T - 1:T, :]
        cum2 = cum * LOG2_E
        cum_ref[rows, :] = cum2
        cumt_ref[:, rows] = cum2.T
        dtt_ref[:, rows] = dt[rows].T
        wendt_ref[:, rows] = (dt[rows] * jnp.exp(cum_last - cum)).T

    def z_pair(t):
        acc = jnp.dot(xn, w_ref[:, t * LANES:t * LANES + pair], preferred_element_type=F32)
        gate = acc * _sigmoid(acc)
        p_ref[t] = gate[:, :LANES]
        p_ref[t + 1] = gate[:, LANES:]

    def conv_silu(raw, kc):
        w = cw_ref[kc]
        slot = kc % N_STAGE
        stage_ref[slot, 0:HALO_ROWS, :] = halo_ref[kc]
        stage_ref[slot, HALO_ROWS:, :] = raw
        acc = cb_ref[kc] + w[SSD_CONV_WIDTH - 1:SSD_CONV_WIDTH] * raw
        for d in range(1, SSD_CONV_WIDTH):
            shifted = stage_ref[slot, HALO_ROWS - d:HALO_ROWS - d + R, :]
            acc = acc + w[SSD_CONV_WIDTH - 1 - d:SSD_CONV_WIDTH - d] * shifted
        halo_ref[kc] = raw[R - HALO_ROWS:]
        return acc * _sigmoid(acc)

    def put_conv(kc, val):
        if kc < B_TILE0:
            p_ref[P_X0 + kc] = val
        elif kc < C_TILE0:
            for c in range(R // T):
                bt_ref[kc - B_TILE0, :, c * T:(c + 1) * T] = val[c * T:(c + 1) * T].T.astype(bt_ref.dtype)
        else:
            c_ref[kc - C_TILE0] = val.astype(c_ref.dtype)

    def conv_pair(t):
        col = (N_Z_TILES + t) * LANES
        acc = jnp.dot(xn, w_ref[:, col:col + pair], preferred_element_type=F32)
        put_conv(t, conv_silu(acc[:, :LANES], t))
        put_conv(t + 1, conv_silu(acc[:, LANES:], t + 1))

    for i in range(N_Z_TILES // 2):
        conv_pair(4 * i)
        z_pair(2 * i)
        conv_pair(4 * i + 2)


def _ssd_in_proj(x2d, gain, w_pad, conv_w, conv_b, dt_bias, a_log):
    m = x2d.shape[0]
    t_major = pl.BlockSpec((ROW_TILE, LANES), lambda i: (i, 0))
    h_major = pl.BlockSpec((LANES, ROW_TILE), lambda i: (0, i))
    return pl.pallas_call(
        _ssd_in_proj_kernel,
        grid=(m // ROW_TILE,),
        in_specs=[
            pl.BlockSpec((ROW_TILE, D_MODEL), lambda i: (i, 0)),
            _resident((1, D_MODEL)),
            _resident((D_MODEL, N_IN_TILES * LANES)),
            _resident((N_CONV_TILES, SSD_CONV_WIDTH, LANES)),
            _resident((N_CONV_TILES, 1, LANES)),
            _resident((1, LANES)),
            _resident((1, LANES)),
        ],
        out_specs=[
            pl.BlockSpec((N_P_TILES, ROW_TILE, LANES), lambda i: (0, i, 0)),
            pl.BlockSpec((SSD_N_GROUPS, ROW_TILE, LANES), lambda i: (0, i, 0)),
            pl.BlockSpec((SSD_N_GROUPS, SSD_D_STATE, ROW_TILE), lambda i: (0, 0, i)),
            t_major, h_major, h_major, h_major,
        ],
        out_shape=[
            jax.ShapeDtypeStruct((N_P_TILES, m, LANES), F32),
            jax.ShapeDtypeStruct((SSD_N_GROUPS, m, LANES), F32),
            jax.ShapeDtypeStruct((SSD_N_GROUPS, SSD_D_STATE, m), F32),
            jax.ShapeDtypeStruct((m, LANES), F32),
            jax.ShapeDtypeStruct((LANES, m), F32),
            jax.ShapeDtypeStruct((LANES, m), F32),
            jax.ShapeDtypeStruct((LANES, m), F32),
        ],
        scratch_shapes=[
            pltpu.VMEM((N_CONV_TILES, HALO_ROWS, LANES), F32),
            pltpu.VMEM((N_STAGE, HALO_ROWS + ROW_TILE, LANES), F32),
        ],
        compiler_params=_params("arbitrary"),
        name="ssd_in_proj",
    )(x2d, gain, w_pad, conv_w, conv_b, dt_bias, a_log)


def _ssd_kernel(p_ref, c_ref, bt_ref, cum_ref, cumt_ref, dtt_ref, wendt_ref, dskip_ref, nw_ref, o_ref, state_ref):
    T = SSD_CHUNK
    lane = lax.broadcasted_iota(jnp.int32, (T, LANES), 1)
    row = lax.broadcasted_iota(jnp.int32, (T, LANES), 0)
    lane_lo = lane < HEAD_DIM
    causal = row >= lane

    @pl.when(pl.program_id(1) == 0)
    def _():
        state_ref[...] = jnp.zeros_like(state_ref)

    def group(rows, dec, g):
        cm = c_ref[g, rows, :].astype(BF16)
        bt = bt_ref[g, :, rows].astype(BF16)
        cb = jnp.dot(cm, bt, preferred_element_type=F32).astype(BF16)
        gated = []
        for half in range(2):
            kx = 2 * g + half
            x = p_ref[P_X0 + kx, rows, :]
            st = state_ref[kx]
            xb = x.astype(BF16)
            lhs_y = []
            lhs_s = []
            for i in range(2):
                h = 2 * kx + i
                crow = cumt_ref[h:h + 1, rows]
                cbc = jnp.broadcast_to(cum_ref[rows, h:h + 1], (T, LANES))
                decay = jnp.where(causal, jnp.exp2(cbc - crow), 0.0).astype(BF16)
                lhs_y.append(jnp.concatenate(
                    [cb * decay * dtt_ref[h:h + 1, rows].astype(BF16), cm * jnp.exp2(cbc).astype(BF16)],
                    axis=1))
                lhs_s.append(bt * wendt_ref[h:h + 1, rows].astype(BF16))
            yy = jnp.dot(jnp.concatenate(lhs_y, axis=0), jnp.concatenate([xb, st.astype(BF16)], axis=0),
                         preferred_element_type=F32)
            y = jnp.where(lane_lo, yy[:T], yy[T:]) + dskip_ref[kx] * x
            ds = jnp.dot(jnp.concatenate(lhs_s, axis=0), xb, preferred_element_type=F32)
            d0 = jnp.broadcast_to(dec[:, 2 * kx:2 * kx + 1], (1, LANES))
            d1 = jnp.broadcast_to(dec[:, 2 * kx + 1:2 * kx + 2], (1, LANES))
            state_ref[kx] = st * jnp.where(lane_lo[0:1], d0, d1) + jnp.where(
                lane_lo, ds[:SSD_D_STATE], ds[SSD_D_STATE:])
            gated.append(y * p_ref[kx, rows, :])
        ss = (jnp.sum(gated[0] * gated[0], axis=-1, keepdims=True)
              + jnp.sum(gated[1] * gated[1], axis=-1, keepdims=True))
        r = lax.rsqrt(ss * (1.0 / (2 * LANES)) + NORM_EPS)
        for half in range(2):
            kx = 2 * g + half
            o_ref[kx, rows, :] = (gated[half] * r * nw_ref[kx]).astype(o_ref.dtype)

    for ci in range(SSD_CHUNKS_PER_STEP):
        rows = slice(ci * T, (ci + 1) * T)
        dec = jnp.exp2(cum_ref[(ci + 1) * T - 1:(ci + 1) * T, :])
        for g in range(SSD_N_GROUPS):
            group(rows, dec, g)


def _ssd_core(p, c, bt, cum, cumt, dtt, wendt, dskip, norm_w, batch):
    m = p.shape[1]
    R = SSD_CHUNKS_PER_STEP * SSD_CHUNK
    n_steps = m // batch // R
    h_major = pl.BlockSpec((LANES, R), lambda b, c: (0, b * n_steps + c))
    return pl.pallas_call(
        _ssd_kernel,
        grid=(batch, n_steps),
        in_specs=[
            pl.BlockSpec((N_P_TILES, R, LANES), lambda b, c: (0, b * n_steps + c, 0)),
            pl.BlockSpec((SSD_N_GROUPS, R, LANES), lambda b, c: (0, b * n_steps + c, 0)),
            pl.BlockSpec((SSD_N_GROUPS, SSD_D_STATE, R), lambda b, c: (0, 0, b * n_steps + c)),
            pl.BlockSpec((R, LANES), lambda b, c: (b * n_steps + c, 0)),
            h_major, h_major, h_major,
            _resident((N_X_TILES, 1, LANES)),
            _resident((N_X_TILES, 1, LANES)),
        ],
        out_specs=pl.BlockSpec((N_X_TILES, R, LANES), lambda b, c: (0, b * n_steps + c, 0)),
        out_shape=jax.ShapeDtypeStruct((N_X_TILES, m, LANES), BF16),
        scratch_shapes=[pltpu.VMEM((N_X_TILES, SSD_D_STATE, LANES), F32)],
        compiler_params=_params("arbitrary", "arbitrary"),
        name="ssd_core",
    )(p, c, bt, cum, cumt, dtt, wendt, dskip, norm_w)


def _mlp_residual(h, g1_ref, wu_ref, wd_ref, g2_ref):
    u = _rms(h, g1_ref[...]).astype(BF16)
    acc = None
    for c in range(0, D_FF, FF_CHUNK):
        a = jnp.dot(u, wu_ref[:, c:c + FF_CHUNK], preferred_element_type=F32)
        a = jnp.square(jnp.maximum(a, 0.0)).astype(BF16)
        part = jnp.dot(a, wd_ref[c:c + FF_CHUNK, :], preferred_element_type=F32)
        acc = part if acc is None else acc + part
    return h + _rms(acc, g2_ref[...])


def _ssd_tail_kernel(y_ref, wo_ref, h_ref, gp_ref, g1_ref, wu_ref, wd_ref, g2_ref,
                     gq_ref, wq_ref, bq_ref, o_ref, qkv_ref):
    y = jnp.concatenate([y_ref[k] for k in range(N_X_TILES)], axis=1)
    mix = jnp.dot(y, wo_ref[...], preferred_element_type=F32)
    h = _mlp_residual(h_ref[...] + _rms(mix, gp_ref[...]), g1_ref, wu_ref, wd_ref, g2_ref)
    o_ref[...] = h
    xn = _rms(h, gq_ref[...]).astype(BF16)
    qkv = jnp.dot(xn, wq_ref[...], preferred_element_type=F32) + bq_ref[...]
    qkv_ref[...] = qkv.astype(qkv_ref.dtype)


def _attn_tail_kernel(a_ref, wo_ref, bo_ref, h_ref, gp_ref, g1_ref, wu_ref, wd_ref, g2_ref, o_ref):
    mix = jnp.dot(a_ref[...], wo_ref[...], preferred_element_type=F32) + bo_ref[...]
    o_ref[...] = _mlp_residual(h_ref[...] + _rms(mix, gp_ref[...]), g1_ref, wu_ref, wd_ref, g2_ref)


_ROWS = pl.BlockSpec((ROW_TILE, D_MODEL), lambda i: (i, 0))
_GAIN = (1, D_MODEL)


def _ssd_tail(y_tiles, w_out, h, g_post, g1, wu, wd, g2, g_qkv, w_qkv, b_qkv):
    m = h.shape[0]
    return pl.pallas_call(
        _ssd_tail_kernel,
        grid=(m // ROW_TILE,),
        in_specs=[
            pl.BlockSpec((N_X_TILES, ROW_TILE, LANES), lambda i: (0, i, 0)),
            _resident(w_out.shape), _ROWS, _resident(_GAIN),
            _resident(_GAIN), _resident(wu.shape), _resident(wd.shape), _resident(_GAIN),
            _resident(_GAIN), _resident(w_qkv.shape), _resident(b_qkv.shape),
        ],
        out_specs=[_ROWS, pl.BlockSpec((ROW_TILE, ATTN_QKV_DIM), lambda i: (i, 0))],
        out_shape=[jax.ShapeDtypeStruct((m, D_MODEL), F32),
                   jax.ShapeDtypeStruct((m, ATTN_QKV_DIM), BF16)],
        compiler_params=_params("arbitrary"),
        name="ssd_tail_mlp_qkv",
    )(y_tiles, w_out, h, g_post, g1, wu, wd, g2, g_qkv, w_qkv, b_qkv)


def _attn_tail(a, w_o, b_o, h, g_post, g1, wu, wd, g2):
    m = h.shape[0]
    return pl.pallas_call(
        _attn_tail_kernel,
        grid=(m // ROW_TILE,),
        in_specs=[
            pl.BlockSpec((ROW_TILE, ATTN_Q_DIM), lambda i: (i, 0)),
            _resident(w_o.shape), _resident(_GAIN), _ROWS, _resident(_GAIN),
            _resident(_GAIN), _resident(wu.shape), _resident(wd.shape), _resident(_GAIN),
        ],
        out_specs=_ROWS,
        out_shape=jax.ShapeDtypeStruct((m, D_MODEL), F32),
        compiler_params=_params("arbitrary"),
        name="attn_tail_mlp",
    )(a, w_o, b_o, h, g_post, g1, wu, wd, g2)


def _attn_kernel(sink_ref, q_ref, kp_ref, kc_ref, vp_ref, vc_ref, o_ref):
    W = ATTN_WINDOW
    PACK = 16
    kmin = jnp.where(pl.program_id(1) == 0, W, 0)
    top = lax.broadcasted_iota(jnp.int32, (PACK, 2 * LANES), 0) == 0
    zero = jnp.zeros((), BF16)

    def without_row0(prev_ref, cur_ref):
        prev = prev_ref[...]
        return jnp.concatenate([jnp.where(top, zero, prev[:PACK]), prev[PACK:], cur_ref[...]], axis=0)

    kcat = without_row0(kp_ref, kc_ref)
    vcat = without_row0(vp_ref, vc_ref)
    qrow = lax.broadcasted_iota(jnp.int32, (2 * W, 2 * W), 0)
    kpos = lax.broadcasted_iota(jnp.int32, (2 * W, 2 * W), 1)
    qpos = jnp.where(qrow >= W, qrow - W, qrow) + W
    rel = qpos - kpos
    valid = ((rel >= 0) & (rel < W) & (kpos >= kmin)) | (kpos == 0)
    bias = jnp.where(valid, 0.0, -jnp.inf)
    lane = lax.broadcasted_iota(jnp.int32, (W, LANES), 1)
    lane_lo = lane < HEAD_DIM
    q_extra = jnp.concatenate([jnp.where(lane < 3, 1.0, 0.0),
                               jnp.where((lane >= 3) & (lane < 6), 1.0, 0.0)], axis=0).astype(BF16)
    lane1 = lax.broadcasted_iota(jnp.int32, (1, LANES), 1)
    is_piece0 = (lane1 == 0) | (lane1 == 3)
    is_piece1 = (lane1 == 1) | (lane1 == 4)
    for t in range(ATTN_Q_DIM // LANES):
        pair = t // ATTN_REP
        qt = q_ref[:, t * LANES:(t + 1) * LANES] * jnp.asarray(HEAD_DIM ** -0.5, BF16)
        qs = jnp.concatenate([jnp.where(lane_lo, qt, zero), jnp.where(lane_lo, zero, qt)], axis=0)
        sv = sink_ref[t]
        s_hi = sv.astype(BF16).astype(F32)
        s_mid = (sv - s_hi).astype(BF16).astype(F32)
        s_lo = (sv - s_hi - s_mid).astype(BF16).astype(F32)
        srow = jnp.where(is_piece0, s_hi, jnp.where(is_piece1, s_mid, s_lo))
        k_extra = jnp.concatenate(
            [jnp.where(top[:, :LANES], jnp.broadcast_to(srow, (PACK, LANES)), 0.0).astype(BF16),
             jnp.zeros((2 * W - PACK, LANES), BF16)], axis=0)
        kt = jnp.concatenate([kcat[:, pair * LANES:(pair + 1) * LANES], k_extra], axis=1)
        vt = vcat[:, pair * LANES:(pair + 1) * LANES]
        s = lax.dot_general(jnp.concatenate([qs, q_extra], axis=1), kt, (((1,), (1,)), ((), ())),
                            preferred_element_type=F32) + bias
        mx = jnp.max(s, axis=-1, keepdims=True)
        e = jnp.exp(s - mx)
        den = jnp.sum(e, axis=-1, keepdims=True)
        pv = jnp.dot(e.astype(BF16), vt, preferred_element_type=F32) * (1.0 / den)
        o_ref[:, t * LANES:(t + 1) * LANES] = jnp.where(lane_lo, pv[:W], pv[W:]).astype(o_ref.dtype)


def _attention(qkv, sinks, batch):
    m = qkv.shape[0]
    nb = m // batch // ATTN_WINDOW
    W = ATTN_WINDOW
    k_col = ATTN_Q_DIM // ATTN_KV_DIM
    v_col = k_col + 1

    def cur(b, i):
        return b * nb + i

    def prev(b, i):
        return jnp.maximum(b * nb + i - 1, 0)

    return pl.pallas_call(
        _attn_kernel,
        grid=(batch, nb),
        in_specs=[
            _resident(sinks.shape),
            pl.BlockSpec((W, ATTN_Q_DIM), lambda b, i: (cur(b, i), 0)),
            pl.BlockSpec((W, ATTN_KV_DIM), lambda b, i: (prev(b, i), k_col)),
            pl.BlockSpec((W, ATTN_KV_DIM), lambda b, i: (cur(b, i), k_col)),
            pl.BlockSpec((W, ATTN_KV_DIM), lambda b, i: (prev(b, i), v_col)),
            pl.BlockSpec((W, ATTN_KV_DIM), lambda b, i: (cur(b, i), v_col)),
        ],
        out_specs=pl.BlockSpec((W, ATTN_Q_DIM), lambda b, i: (cur(b, i), 0)),
        out_shape=jax.ShapeDtypeStruct((m, ATTN_Q_DIM), BF16),
        compiler_params=_params("arbitrary", "arbitrary"),
        name="swa_sink_attention",
    )(sinks, qkv, qkv, qkv, qkv, qkv)


def _permute_q_heads(v, axis):
    inner = v.shape[axis] // ATTN_N_Q_HEADS
    shape = v.shape[:axis] + (ATTN_N_KV_HEADS // 2, 2, ATTN_REP, inner) + v.shape[axis + 1:]
    order = list(range(len(shape)))
    order[axis + 1], order[axis + 2] = axis + 2, axis + 1
    return v.reshape(shape).transpose(order).reshape(v.shape)


def _lane_tiles(v, n_tiles):
    return v.astype(F32).reshape(n_tiles, 1, LANES)


def kernel(x, ssd_w_in, ssd_conv_w, ssd_conv_b, ssd_dt_bias, ssd_a_log, ssd_d, ssd_norm_w, ssd_w_out, attn_w_qkv, attn_b_qkv, attn_sinks, attn_w_o, attn_b_o, mlp_w_up, mlp_w_down, mix_pre_norm, mix_post_norm, ffn_pre_norm, ffn_post_norm):
    batch, seq, d = x.shape
    m = batch * seq
    h = x.reshape(m, d)
    row = lambda v: v.astype(F32).reshape(1, -1)

    w_in = jnp.pad(ssd_w_in[0], ((0, 0), (0, N_IN_TILES * LANES - SSD_IN_DIM))).astype(BF16)
    conv_w = ssd_conv_w[0].astype(F32).reshape(SSD_CONV_WIDTH, N_CONV_TILES, LANES).transpose(1, 0, 2)
    pad_heads = lambda v: jnp.pad(v.astype(F32), (0, LANES - SSD_N_HEADS)).reshape(1, LANES)
    assert seq == SEQ and seq % ROW_TILE == 0
    p, c, bt, cum, cumt, dtt, wendt = _ssd_in_proj(
        h, row(mix_pre_norm[0]), w_in, conv_w, _lane_tiles(ssd_conv_b[0], N_CONV_TILES),
        pad_heads(ssd_dt_bias[0]), pad_heads(ssd_a_log[0]))
    y = _ssd_core(
        p, c, bt, cum, cumt, dtt, wendt, _lane_tiles(jnp.repeat(ssd_d[0], HEAD_DIM), N_X_TILES),
        _lane_tiles(ssd_norm_w[0], N_X_TILES), batch)
    wq, wkv = attn_w_qkv[0][:, :ATTN_Q_DIM], attn_w_qkv[0][:, ATTN_Q_DIM:]
    bq, bkv = attn_b_qkv[0][:ATTN_Q_DIM], attn_b_qkv[0][ATTN_Q_DIM:]
    w_qkv = jnp.concatenate([_permute_q_heads(wq, 1), wkv], axis=1).astype(BF16)
    b_qkv = jnp.concatenate([_permute_q_heads(bq, 0), bkv]).astype(F32).reshape(1, -1)
    h, qkv = _ssd_tail(
        y, ssd_w_out[0].astype(BF16), h, row(mix_post_norm[0]),
        row(ffn_pre_norm[0]), mlp_w_up[0].astype(BF16), mlp_w_down[0].astype(BF16), row(ffn_post_norm[0]),
        row(mix_pre_norm[1]), w_qkv, b_qkv)

    sink_pairs = _permute_q_heads(attn_sinks[0], 0).astype(F32).reshape(ATTN_Q_DIM // LANES, 2, 1)
    sink_rows = jnp.pad(jnp.broadcast_to(sink_pairs, sink_pairs.shape[:2] + (3,)).reshape(-1, 1, 6),
                        ((0, 0), (0, 0), (0, LANES - 6)))
    a = _attention(qkv, sink_rows, batch)
    h = _attn_tail(
        a, _permute_q_heads(attn_w_o[0], 0).astype(BF16), row(attn_b_o[0]), h, row(mix_post_norm[1]),
        row(ffn_pre_norm[1]), mlp_w_up[1].astype(BF16), mlp_w_down[1].astype(BF16), row(ffn_post_norm[1]))
    return h.reshape(batch, seq, d)
```

```python
import functools

import jax
import jax.numpy as jnp
from jax import lax
from jax.experimental import pallas as pl
from jax.experimental.pallas import tpu as pltpu

F32 = jnp.float32
BF16 = jnp.bfloat16

D_MODEL = 1024
SEQ = 4096
NORM_EPS = 1e-6
LOG2_E = 1.4426950408889634

LANES = 128
HEAD_DIM = 64

SSD_D_INNER = 2048
SSD_N_HEADS = 32
SSD_N_GROUPS = 8
SSD_D_STATE = 128
SSD_CONV_WIDTH = 4
SSD_CHUNK = 128
SSD_CONV_DIM = SSD_D_INNER + 2 * SSD_N_GROUPS * SSD_D_STATE
SSD_IN_DIM = SSD_D_INNER + SSD_CONV_DIM + SSD_N_HEADS
N_Z_TILES = SSD_D_INNER // LANES
N_X_TILES = SSD_D_INNER // LANES
N_CONV_TILES = SSD_CONV_DIM // LANES
N_IN_TILES = N_Z_TILES + N_CONV_TILES + 1
B_TILE0 = N_X_TILES
C_TILE0 = N_X_TILES + SSD_N_GROUPS
HALO_ROWS = 8
SSD_CHUNKS_PER_STEP = 4
N_STAGE = 4
P_X0 = N_Z_TILES
N_P_TILES = N_Z_TILES + N_X_TILES

ATTN_N_Q_HEADS = 16
ATTN_N_KV_HEADS = 4
ATTN_REP = 4
ATTN_WINDOW = 128
ATTN_Q_DIM = ATTN_N_Q_HEADS * HEAD_DIM
ATTN_KV_DIM = ATTN_N_KV_HEADS * HEAD_DIM
ATTN_QKV_DIM = ATTN_Q_DIM + 2 * ATTN_KV_DIM

D_FF = 4096
FF_CHUNK = 1024

ROW_TILE = 512
VMEM_LIMIT = 56 * 1024 * 1024


def _rms(x, gain):
    ms = jnp.mean(x * x, axis=-1, keepdims=True)
    return x * lax.rsqrt(ms + NORM_EPS) * gain


def _sigmoid(x):
    return 1.0 / (1.0 + jnp.exp2(x * (-LOG2_E)))


def _resident(shape):
    nd = len(shape)
    return pl.BlockSpec(shape, lambda *_: (0,) * nd, pipeline_mode=pl.Buffered(1))


def _resident_layer(stacked_shape, layer):
    return pl.BlockSpec((None,) + tuple(stacked_shape[1:]), lambda *_: (layer, 0, 0),
                        pipeline_mode=pl.Buffered(1))


def _params(*sem):
    return pltpu.CompilerParams(dimension_semantics=sem, vmem_limit_bytes=VMEM_LIMIT)


def _ssd_in_proj_kernel(x_ref, g_ref, w_ref, wdt_ref, cw_ref, cb_ref, dtb_ref, alog_ref,
                        p_ref, c_ref, bt_ref, cum_ref, cumt_ref, dtt_ref, wendt_ref, halo_ref, stage_ref):
    R = ROW_TILE
    T = SSD_CHUNK

    @pl.when(pl.program_id(0) % (SEQ // ROW_TILE) == 0)
    def _():
        halo_ref[...] = jnp.zeros_like(halo_ref)

    xn = _rms(x_ref[...], g_ref[...]).astype(BF16)
    pair = 2 * LANES

    dt_in = jnp.dot(xn, wdt_ref[...], preferred_element_type=F32) + dtb_ref[...]
    dt = jnp.maximum(dt_in, 0.0) + jnp.log1p(jnp.exp(-jnp.abs(dt_in)))
    aq = dt * (-jnp.exp(alog_ref[...]))
    tri = (lax.broadcasted_iota(jnp.int32, (T, T), 0)
           >= lax.broadcasted_iota(jnp.int32, (T, T), 1)).astype(BF16)
    aq_hi = aq.astype(BF16)
    rem = aq - aq_hi.astype(F32)
    aq_mid = rem.astype(BF16)
    aq_lo = (rem - aq_mid.astype(F32)).astype(BF16)
    for c in range(R // T):
        rows = slice(c * T, (c + 1) * T)
        cum = (jnp.dot(tri, aq_hi[rows], preferred_element_type=F32)
               + jnp.dot(tri, aq_mid[rows], preferred_element_type=F32)
               + jnp.dot(tri, aq_lo[rows], preferred_element_type=F32))
        cum_last = cum[T - 1:T, :]
        cum2 = cum * LOG2_E
        cum_ref[rows, :] = cum2
        cumt_ref[:, rows] = cum2.T
        dtt_ref[:, rows] = dt[rows].T
        wendt_ref[:, rows] = (dt[rows] * jnp.exp(cum_last - cum)).T

    def z_pair(t):
        acc = jnp.dot(xn, w_ref[:, t * LANES:t * LANES + pair], preferred_element_type=F32)
        gate = acc * _sigmoid(acc)
        p_ref[t] = gate[:, :LANES]
        p_ref[t + 1] = gate[:, LANES:]

    def conv_silu(raw, kc):
        w = cw_ref[kc]
        slot = kc % N_STAGE
        stage_ref[slot, 0:HALO_ROWS, :] = halo_ref[kc]
        stage_ref[slot, HALO_ROWS:, :] = raw
        acc = cb_ref[kc] + w[SSD_CONV_WIDTH - 1:SSD_CONV_WIDTH] * raw
        for d in range(1, SSD_CONV_WIDTH):
            shifted = stage_ref[slot, HALO_ROWS - d:HALO_ROWS - d + R, :]
            acc = acc + w[SSD_CONV_WIDTH - 1 - d:SSD_CONV_WIDTH - d] * shifted
        halo_ref[kc] = raw[R - HALO_ROWS:]
        return acc * _sigmoid(acc)

    def put_conv(kc, val):
        if kc < B_TILE0:
            p_ref[P_X0 + kc] = val
        elif kc < C_TILE0:
            for c in range(R // T):
                bt_ref[kc - B_TILE0, :, c * T:(c + 1) * T] = val[c * T:(c + 1) * T].T.astype(bt_ref.dtype)
        else:
            c_ref[kc - C_TILE0] = val.astype(c_ref.dtype)

    def conv_pair(t):
        col = (N_Z_TILES + t) * LANES
        acc = jnp.dot(xn, w_ref[:, col:col + pair], preferred_element_type=F32)
        put_conv(t, conv_silu(acc[:, :LANES], t))
        put_conv(t + 1, conv_silu(acc[:, LANES:], t + 1))

    for i in range(N_Z_TILES // 2):
        conv_pair(4 * i)
        z_pair(2 * i)
        conv_pair(4 * i + 2)


def _ssd_in_proj(x2d, gain, w_in, w_dt, conv_w, conv_b, dt_bias, a_log):
    m = x2d.shape[0]
    t_major = pl.BlockSpec((ROW_TILE, LANES), lambda i: (i, 0))
    h_major = pl.BlockSpec((LANES, ROW_TILE), lambda i: (0, i))
    return pl.pallas_call(
        _ssd_in_proj_kernel,
        grid=(m // ROW_TILE,),
        in_specs=[
            pl.BlockSpec((ROW_TILE, D_MODEL), lambda i: (i, 0)),
            _resident((1, D_MODEL)),
            _resident(w_in.shape),
            _resident(w_dt.shape),
            _resident((N_CONV_TILES, SSD_CONV_WIDTH, LANES)),
            _resident((N_CONV_TILES, 1, LANES)),
            _resident((1, LANES)),
            _resident((1, LANES)),
        ],
        out_specs=[
            pl.BlockSpec((N_P_TILES, ROW_TILE, LANES), lambda i: (0, i, 0)),
            pl.BlockSpec((SSD_N_GROUPS, ROW_TILE, LANES), lambda i: (0, i, 0)),
            pl.BlockSpec((SSD_N_GROUPS, SSD_D_STATE, ROW_TILE), lambda i: (0, 0, i)),
            t_major, h_major, h_major, h_major,
        ],
        out_shape=[
            jax.ShapeDtypeStruct((N_P_TILES, m, LANES), F32),
            jax.ShapeDtypeStruct((SSD_N_GROUPS, m, LANES), F32),
            jax.ShapeDtypeStruct((SSD_N_GROUPS, SSD_D_STATE, m), F32),
            jax.ShapeDtypeStruct((m, LANES), F32),
            jax.ShapeDtypeStruct((LANES, m), F32),
            jax.ShapeDtypeStruct((LANES, m), F32),
            jax.ShapeDtypeStruct((LANES, m), F32),
        ],
        scratch_shapes=[
            pltpu.VMEM((N_CONV_TILES, HALO_ROWS, LANES), F32),
            pltpu.VMEM((N_STAGE, HALO_ROWS + ROW_TILE, LANES), F32),
        ],
        compiler_params=_params("arbitrary"),
        name="ssd_in_proj",
    )(x2d, gain, w_in, w_dt, conv_w, conv_b, dt_bias, a_log)


def _ssd_kernel(p_ref, c_ref, bt_ref, cum_ref, cumt_ref, dtt_ref, wendt_ref, dskip_ref, nw_ref, o_ref, state_ref):
    T = SSD_CHUNK
    lane = lax.broadcasted_iota(jnp.int32, (T, LANES), 1)
    row = lax.broadcasted_iota(jnp.int32, (T, LANES), 0)
    lane_lo = lane < HEAD_DIM
    causal = row >= lane

    @pl.when(pl.program_id(1) == 0)
    def _():
        state_ref[...] = jnp.zeros_like(state_ref)

    def group(rows, dec, g):
        cm = c_ref[g, rows, :].astype(BF16)
        bt = bt_ref[g, :, rows].astype(BF16)
        cb = jnp.dot(cm, bt, preferred_element_type=F32).astype(BF16)
        gated = []
        for half in range(2):
            kx = 2 * g + half
            x = p_ref[P_X0 + kx, rows, :]
            st = state_ref[kx]
            xb = x.astype(BF16)
            lhs_y = []
            lhs_s = []
            for i in range(2):
                h = 2 * kx + i
                crow = cumt_ref[h:h + 1, rows]
                cbc = jnp.broadcast_to(cum_ref[rows, h:h + 1], (T, LANES))
                decay = jnp.where(causal, jnp.exp2(cbc - crow), 0.0).astype(BF16)
                lhs_y.append(jnp.concatenate(
                    [cb * decay * dtt_ref[h:h + 1, rows].astype(BF16), cm * jnp.exp2(cbc).astype(BF16)],
                    axis=1))
                lhs_s.append(bt * wendt_ref[h:h + 1, rows].astype(BF16))
            yy = jnp.dot(jnp.concatenate(lhs_y, axis=0), jnp.concatenate([xb, st.astype(BF16)], axis=0),
                         preferred_element_type=F32)
            y = jnp.where(lane_lo, yy[:T], yy[T:]) + dskip_ref[kx] * x
            ds = jnp.dot(jnp.concatenate(lhs_s, axis=0), xb, preferred_element_type=F32)
            d0 = jnp.broadcast_to(dec[:, 2 * kx:2 * kx + 1], (1, LANES))
            d1 = jnp.broadcast_to(dec[:, 2 * kx + 1:2 * kx + 2], (1, LANES))
            state_ref[kx] = st * jnp.where(lane_lo[0:1], d0, d1) + jnp.where(
                lane_lo, ds[:SSD_D_STATE], ds[SSD_D_STATE:])
            gated.append(y * p_ref[kx, rows, :])
        ss = (jnp.sum(gated[0] * gated[0], axis=-1, keepdims=True)
              + jnp.sum(gated[1] * gated[1], axis=-1, keepdims=True))
        r = lax.rsqrt(ss * (1.0 / (2 * LANES)) + NORM_EPS)
        for half in range(2):
            kx = 2 * g + half
            o_ref[kx, rows, :] = (gated[half] * r * nw_ref[kx]).astype(o_ref.dtype)

    for ci in range(SSD_CHUNKS_PER_STEP):
        rows = slice(ci * T, (ci + 1) * T)
        dec = jnp.exp2(cum_ref[(ci + 1) * T - 1:(ci + 1) * T, :])
        for g in range(SSD_N_GROUPS):
            group(rows, dec, g)


def _ssd_core(p, c, bt, cum, cumt, dtt, wendt, dskip, norm_w, batch):
    m = p.shape[1]
    R = SSD_CHUNKS_PER_STEP * SSD_CHUNK
    n_steps = m // batch // R
    h_major = pl.BlockSpec((LANES, R), lambda b, c: (0, b * n_steps + c))
    return pl.pallas_call(
        _ssd_kernel,
        grid=(batch, n_steps),
        in_specs=[
            pl.BlockSpec((N_P_TILES, R, LANES), lambda b, c: (0, b * n_steps + c, 0)),
            pl.BlockSpec((SSD_N_GROUPS, R, LANES), lambda b, c: (0, b * n_steps + c, 0)),
            pl.BlockSpec((SSD_N_GROUPS, SSD_D_STATE, R), lambda b, c: (0, 0, b * n_steps + c)),
            pl.BlockSpec((R, LANES), lambda b, c: (b * n_steps + c, 0)),
            h_major, h_major, h_major,
            _resident((N_X_TILES, 1, LANES)),
            _resident((N_X_TILES, 1, LANES)),
        ],
        out_specs=pl.BlockSpec((N_X_TILES, R, LANES), lambda b, c: (0, b * n_steps + c, 0)),
        out_shape=jax.ShapeDtypeStruct((N_X_TILES, m, LANES), BF16),
        scratch_shapes=[pltpu.VMEM((N_X_TILES, SSD_D_STATE, LANES), F32)],
        compiler_params=_params("arbitrary", "arbitrary"),
        name="ssd_core",
    )(p, c, bt, cum, cumt, dtt, wendt, dskip, norm_w)


def _mlp_residual(h, g1_ref, wu_ref, wd_ref, g2_ref):
    u = _rms(h, g1_ref[...]).astype(BF16)
    acc = None
    for c in range(0, D_FF, FF_CHUNK):
        a = jnp.dot(u, wu_ref[:, c:c + FF_CHUNK], preferred_element_type=F32)
        a = jnp.square(jnp.maximum(a, 0.0)).astype(BF16)
        part = jnp.dot(a, wd_ref[c:c + FF_CHUNK, :], preferred_element_type=F32)
        acc = part if acc is None else acc + part
    return h + _rms(acc, g2_ref[...])


def _ssd_tail_kernel(y_ref, wo_ref, h_ref, gp_ref, g1_ref, wu_ref, wd_ref, g2_ref,
                     gq_ref, wq_ref, bq_ref, o_ref, qkv_ref):
    y = jnp.concatenate([y_ref[k] for k in range(N_X_TILES)], axis=1)
    mix = jnp.dot(y, wo_ref[...], preferred_element_type=F32)
    h = _mlp_residual(h_ref[...] + _rms(mix, gp_ref[...]), g1_ref, wu_ref, wd_ref, g2_ref)
    o_ref[...] = h
    xn = _rms(h, gq_ref[...]).astype(BF16)
    qkv = jnp.dot(xn, wq_ref[...], preferred_element_type=F32) + bq_ref[...]
    qkv_ref[...] = qkv.astype(qkv_ref.dtype)


def _attn_tail_kernel(a_ref, wo_ref, bo_ref, h_ref, gp_ref, g1_ref, wu_ref, wd_ref, g2_ref, o_ref):
    mix = jnp.dot(a_ref[...], wo_ref[...], preferred_element_type=F32) + bo_ref[...]
    o_ref[...] = _mlp_residual(h_ref[...] + _rms(mix, gp_ref[...]), g1_ref, wu_ref, wd_ref, g2_ref)


_ROWS = pl.BlockSpec((ROW_TILE, D_MODEL), lambda i: (i, 0))
_GAIN = (1, D_MODEL)


def _ssd_tail(y_tiles, w_out, h, g_post, g1, wu, wd, layer, g2, g_qkv, w_qkv, b_qkv):
    m = h.shape[0]
    return pl.pallas_call(
        _ssd_tail_kernel,
        grid=(m // ROW_TILE,),
        in_specs=[
            pl.BlockSpec((N_X_TILES, ROW_TILE, LANES), lambda i: (0, i, 0)),
            _resident(w_out.shape), _ROWS, _resident(_GAIN),
            _resident(_GAIN), _resident_layer(wu.shape, layer), _resident_layer(wd.shape, layer), _resident(_GAIN),
            _resident(_GAIN), _resident(w_qkv.shape), _resident(b_qkv.shape),
        ],
        out_specs=[_ROWS, pl.BlockSpec((ROW_TILE, ATTN_QKV_DIM), lambda i: (i, 0))],
        out_shape=[jax.ShapeDtypeStruct((m, D_MODEL), F32),
                   jax.ShapeDtypeStruct((m, ATTN_QKV_DIM), BF16)],
        compiler_params=_params("arbitrary"),
        name="ssd_tail_mlp_qkv",
    )(y_tiles, w_out, h, g_post, g1, wu, wd, g2, g_qkv, w_qkv, b_qkv)


def _attn_tail(a, w_o, b_o, h, g_post, g1, wu, wd, layer, g2):
    m = h.shape[0]
    return pl.pallas_call(
        _attn_tail_kernel,
        grid=(m // ROW_TILE,),
        in_specs=[
            pl.BlockSpec((ROW_TILE, ATTN_Q_DIM), lambda i: (i, 0)),
            _resident(w_o.shape), _resident(_GAIN), _ROWS, _resident(_GAIN),
            _resident(_GAIN), _resident_layer(wu.shape, layer), _resident_layer(wd.shape, layer), _resident(_GAIN),
        ],
        out_specs=_ROWS,
        out_shape=jax.ShapeDtypeStruct((m, D_MODEL), F32),
        compiler_params=_params("arbitrary"),
        name="attn_tail_mlp",
    )(a, w_o, b_o, h, g_post, g1, wu, wd, g2)


def _attn_kernel(sink_ref, q_ref, kp_ref, kc_ref, vp_ref, vc_ref, o_ref):
    W = ATTN_WINDOW
    PACK = 16
    kmin = jnp.where(pl.program_id(1) == 0, W, 0)
    top = lax.broadcasted_iota(jnp.int32, (PACK, 2 * LANES), 0) == 0
    zero = jnp.zeros((), BF16)

    def without_row0(prev_ref, cur_ref):
        prev = prev_ref[...]
        return jnp.concatenate([jnp.where(top, zero, prev[:PACK]), prev[PACK:], cur_ref[...]], axis=0)

    kcat = without_row0(kp_ref, kc_ref)
    vcat = without_row0(vp_ref, vc_ref)
    qrow = lax.broadcasted_iota(jnp.int32, (2 * W, 2 * W), 0)
    kpos = lax.broadcasted_iota(jnp.int32, (2 * W, 2 * W), 1)
    qpos = jnp.where(qrow >= W, qrow - W, qrow) + W
    rel = qpos - kpos
    valid = ((rel >= 0) & (rel < W) & (kpos >= kmin)) | (kpos == 0)
    bias = jnp.where(valid, 0.0, -jnp.inf)
    lane = lax.broadcasted_iota(jnp.int32, (W, LANES), 1)
    lane_lo = lane < HEAD_DIM
    q_extra = jnp.concatenate([jnp.where(lane < 3, 1.0, 0.0),
                               jnp.where((lane >= 3) & (lane < 6), 1.0, 0.0)], axis=0).astype(BF16)
    lane1 = lax.broadcasted_iota(jnp.int32, (1, LANES), 1)
    is_piece0 = (lane1 == 0) | (lane1 == 3)
    is_piece1 = (lane1 == 1) | (lane1 == 4)
    for t in range(ATTN_Q_DIM // LANES):
        pair = t // ATTN_REP
        qt = q_ref[:, t * LANES:(t + 1) * LANES] * jnp.asarray(HEAD_DIM ** -0.5, BF16)
        qs = jnp.concatenate([jnp.where(lane_lo, qt, zero), jnp.where(lane_lo, zero, qt)], axis=0)
        sv = sink_ref[t]
        s_hi = sv.astype(BF16).astype(F32)
        s_mid = (sv - s_hi).astype(BF16).astype(F32)
        s_lo = (sv - s_hi - s_mid).astype(BF16).astype(F32)
        srow = jnp.where(is_piece0, s_hi, jnp.where(is_piece1, s_mid, s_lo))
        k_extra = jnp.concatenate(
            [jnp.where(top[:, :LANES], jnp.broadcast_to(srow, (PACK, LANES)), 0.0).astype(BF16),
             jnp.zeros((2 * W - PACK, LANES), BF16)], axis=0)
        kt = jnp.concatenate([kcat[:, pair * LANES:(pair + 1) * LANES], k_extra], axis=1)
        vt = vcat[:, pair * LANES:(pair + 1) * LANES]
        s = lax.dot_general(jnp.concatenate([qs, q_extra], axis=1), kt, (((1,), (1,)), ((), ())),
                            preferred_element_type=F32) + bias
        mx = jnp.max(s, axis=-1, keepdims=True)
        e = jnp.exp(s - mx)
        den = jnp.sum(e, axis=-1, keepdims=True)
        pv = jnp.dot(e.astype(BF16), vt, preferred_element_type=F32) * (1.0 / den)
        o_ref[:, t * LANES:(t + 1) * LANES] = jnp.where(lane_lo, pv[:W], pv[W:]).astype(o_ref.dtype)


def _attention(qkv, sinks, batch):
    m = qkv.shape[0]
    nb = m // batch // ATTN_WINDOW
    W = ATTN_WINDOW
    k_col = ATTN_Q_DIM // ATTN_KV_DIM
    v_col = k_col + 1

    def cur(b, i):
        return b * nb + i

    def prev(b, i):
        return jnp.maximum(b * nb + i - 1, 0)

    return pl.pallas_call(
        _attn_kernel,
        grid=(batch, nb),
        in_specs=[
            _resident(sinks.shape),
            pl.BlockSpec((W, ATTN_Q_DIM), lambda b, i: (cur(b, i), 0)),
            pl.BlockSpec((W, ATTN_KV_DIM), lambda b, i: (prev(b, i), k_col)),
            pl.BlockSpec((W, ATTN_KV_DIM), lambda b, i: (cur(b, i), k_col)),
            pl.BlockSpec((W, ATTN_KV_DIM), lambda b, i: (prev(b, i), v_col)),
            pl.BlockSpec((W, ATTN_KV_DIM), lambda b, i: (cur(b, i), v_col)),
        ],
        out_specs=pl.BlockSpec((W, ATTN_Q_DIM), lambda b, i: (cur(b, i), 0)),
        out_shape=jax.ShapeDtypeStruct((m, ATTN_Q_DIM), BF16),
        compiler_params=_params("arbitrary", "arbitrary"),
        name="swa_sink_attention",
    )(sinks, qkv, qkv, qkv, qkv, qkv)


def _permute_q_heads(v, axis):
    inner = v.shape[axis] // ATTN_N_Q_HEADS
    shape = v.shape[:axis] + (ATTN_N_KV_HEADS // 2, 2, ATTN_REP, inner) + v.shape[axis + 1:]
    order = list(range(len(shape)))
    order[axis + 1], order[axis + 2] = axis + 2, axis + 1
    return v.reshape(shape).transpose(order).reshape(v.shape)


def _lane_tiles(v, n_tiles):
    return v.astype(F32).reshape(n_tiles, 1, LANES)


def kernel(x, ssd_w_in, ssd_conv_w, ssd_conv_b, ssd_dt_bias, ssd_a_log, ssd_d, ssd_norm_w, ssd_w_out, attn_w_qkv, attn_b_qkv, attn_sinks, attn_w_o, attn_b_o, mlp_w_up, mlp_w_down, mix_pre_norm, mix_post_norm, ffn_pre_norm, ffn_post_norm):
    batch, seq, d = x.shape
    m = batch * seq
    h = x.reshape(m, d)
    row = lambda v: v.astype(F32).reshape(1, -1)

    w_in = ssd_w_in[0].astype(BF16)
    w_dt = jnp.pad(ssd_w_in[0][:, SSD_IN_DIM - SSD_N_HEADS:], ((0, 0), (0, LANES - SSD_N_HEADS))).astype(BF16)
    conv_w = ssd_conv_w[0].astype(F32).reshape(SSD_CONV_WIDTH, N_CONV_TILES, LANES).transpose(1, 0, 2)
    pad_heads = lambda v: jnp.pad(v.astype(F32), (0, LANES - SSD_N_HEADS)).reshape(1, LANES)
    assert seq == SEQ and seq % ROW_TILE == 0
    p, c, bt, cum, cumt, dtt, wendt = _ssd_in_proj(
        h, row(mix_pre_norm[0]), w_in, w_dt, conv_w, _lane_tiles(ssd_conv_b[0], N_CONV_TILES),
        pad_heads(ssd_dt_bias[0]), pad_heads(ssd_a_log[0]))
    y = _ssd_core(
        p, c, bt, cum, cumt, dtt, wendt, _lane_tiles(jnp.repeat(ssd_d[0], HEAD_DIM), N_X_TILES),
        _lane_tiles(ssd_norm_w[0], N_X_TILES), batch)
    wq, wkv = attn_w_qkv[0][:, :ATTN_Q_DIM], attn_w_qkv[0][:, ATTN_Q_DIM:]
    bq, bkv = attn_b_qkv[0][:ATTN_Q_DIM], attn_b_qkv[0][ATTN_Q_DIM:]
    w_qkv = jnp.concatenate([_permute_q_heads(wq, 1), wkv], axis=1).astype(BF16)
    b_qkv = jnp.concatenate([_permute_q_heads(bq, 0), bkv]).astype(F32).reshape(1, -1)
    w_up, w_down = mlp_w_up.astype(BF16), mlp_w_down.astype(BF16)
    h, qkv = _ssd_tail(
        y, ssd_w_out[0].astype(BF16), h, row(mix_post_norm[0]),
        row(ffn_pre_norm[0]), w_up, w_down, 0, row(ffn_post_norm[0]),
        row(mix_pre_norm[1]), w_qkv, b_qkv)

    sink_pairs = _permute_q_heads(attn_sinks[0], 0).astype(F32).reshape(ATTN_Q_DIM // LANES, 2, 1)
    sink_rows = jnp.pad(jnp.broadcast_to(sink_pairs, sink_pairs.shape[:2] + (3,)).reshape(-1, 1, 6),
                        ((0, 0), (0, 0), (0, LANES - 6)))
    a = _attention(qkv, sink_rows, batch)
    h = _attn_tail(
        a, _permute_q_heads(attn_w_o[0], 0).astype(BF16), row(attn_b_o[0]), h, row(mix_post_norm[1]),
        row(ffn_pre_norm[1]), w_up, w_down, 1, row(ffn_post_norm[1]))
    return h.reshape(batch, seq, d)
```

```python
import functools

import jax
import jax.numpy as jnp
from jax import lax
from jax.experimental import pallas as pl
from jax.experimental.pallas import tpu as pltpu

F32 = jnp.float32
BF16 = jnp.bfloat16

D_MODEL = 1024
SEQ = 4096
NORM_EPS = 1e-6
LOG2_E = 1.4426950408889634

LANES = 128
HEAD_DIM = 64

SSD_D_INNER = 2048
SSD_N_HEADS = 32
SSD_N_GROUPS = 8
SSD_D_STATE = 128
SSD_CONV_WIDTH = 4
SSD_CHUNK = 128
SSD_CONV_DIM = SSD_D_INNER + 2 * SSD_N_GROUPS * SSD_D_STATE
SSD_IN_DIM = SSD_D_INNER + SSD_CONV_DIM + SSD_N_HEADS
N_Z_TILES = SSD_D_INNER // LANES
N_X_TILES = SSD_D_INNER // LANES
N_CONV_TILES = SSD_CONV_DIM // LANES
N_IN_TILES = N_Z_TILES + N_CONV_TILES + 1
B_TILE0 = N_X_TILES
C_TILE0 = N_X_TILES + SSD_N_GROUPS
HALO_ROWS = 8
SSD_CHUNKS_PER_STEP = 4
N_STAGE = 4
P_X0 = N_Z_TILES
N_P_TILES = N_Z_TILES + N_X_TILES

ATTN_N_Q_HEADS = 16
ATTN_N_KV_HEADS = 4
ATTN_REP = 4
ATTN_WINDOW = 128
ATTN_BLOCKS_PER_STEP = 8
ATTN_Q_DIM = ATTN_N_Q_HEADS * HEAD_DIM
ATTN_KV_DIM = ATTN_N_KV_HEADS * HEAD_DIM
ATTN_QKV_DIM = ATTN_Q_DIM + 2 * ATTN_KV_DIM

D_FF = 4096
FF_CHUNK = 1024

ROW_TILE = 512
VMEM_LIMIT = 56 * 1024 * 1024


def _rms(x, gain):
    ms = jnp.mean(x * x, axis=-1, keepdims=True)
    return x * lax.rsqrt(ms + NORM_EPS) * gain


def _sigmoid(x):
    return 1.0 / (1.0 + jnp.exp2(x * (-LOG2_E)))


def _resident(shape):
    nd = len(shape)
    return pl.BlockSpec(shape, lambda *_: (0,) * nd, pipeline_mode=pl.Buffered(1))


def _resident_layer(stacked_shape, layer):
    return pl.BlockSpec((None,) + tuple(stacked_shape[1:]), lambda *_: (layer, 0, 0),
                        pipeline_mode=pl.Buffered(1))


def _params(*sem):
    return pltpu.CompilerParams(dimension_semantics=sem, vmem_limit_bytes=VMEM_LIMIT)


def _ssd_in_proj_kernel(x_ref, g_ref, w_ref, wdt_ref, cw_ref, cb_ref, dtb_ref, alog_ref,
                        p_ref, c_ref, bt_ref, cum_ref, cumt_ref, dtt_ref, wendt_ref, halo_ref, stage_ref):
    R = ROW_TILE
    T = SSD_CHUNK

    @pl.when(pl.program_id(0) % (SEQ // ROW_TILE) == 0)
    def _():
        halo_ref[...] = jnp.zeros_like(halo_ref)

    xn = _rms(x_ref[...], g_ref[...]).astype(BF16)
    pair = 2 * LANES

    dt_in = jnp.dot(xn, wdt_ref[...], preferred_element_type=F32) + dtb_ref[...]
    dt = jnp.maximum(dt_in, 0.0) + jnp.log1p(jnp.exp(-jnp.abs(dt_in)))
    aq = dt * (-jnp.exp(alog_ref[...]))
    tri = (lax.broadcasted_iota(jnp.int32, (T, T), 0)
           >= lax.broadcasted_iota(jnp.int32, (T, T), 1)).astype(BF16)
    aq_hi = aq.astype(BF16)
    rem = aq - aq_hi.astype(F32)
    aq_mid = rem.astype(BF16)
    aq_lo = (rem - aq_mid.astype(F32)).astype(BF16)
    for c in range(R // T):
        rows = slice(c * T, (c + 1) * T)
        cum = (jnp.dot(tri, aq_hi[rows], preferred_element_type=F32)
               + jnp.dot(tri, aq_mid[rows], preferred_element_type=F32)
               + jnp.dot(tri, aq_lo[rows], preferred_element_type=F32))
        cum_last = cum[T - 1:T, :]
        cum2 = cum * LOG2_E
        cum_ref[rows, :] = cum2
        cumt_ref[:, rows] = cum2.T
        dtt_ref[:, rows] = dt[rows].T
        wendt_ref[:, rows] = (dt[rows] * jnp.exp(cum_last - cum)).T

    def z_pair(t):
        acc = jnp.dot(xn, w_ref[:, t * LANES:t * LANES + pair], preferred_element_type=F32)
        gate = acc * _sigmoid(acc)
        p_ref[t] = gate[:, :LANES]
        p_ref[t + 1] = gate[:, LANES:]

    def conv_silu(raw, kc):
        w = cw_ref[kc]
        slot = kc % N_STAGE
        stage_ref[slot, 0:HALO_ROWS, :] = halo_ref[kc]
        stage_ref[slot, HALO_ROWS:, :] = raw
        acc = cb_ref[kc] + w[SSD_CONV_WIDTH - 1:SSD_CONV_WIDTH] * raw
        for d in range(1, SSD_CONV_WIDTH):
            shifted = stage_ref[slot, HALO_ROWS - d:HALO_ROWS - d + R, :]
            acc = acc + w[SSD_CONV_WIDTH - 1 - d:SSD_CONV_WIDTH - d] * shifted
        halo_ref[kc] = raw[R - HALO_ROWS:]
        return acc * _sigmoid(acc)

    def put_conv(kc, val):
        if kc < B_TILE0:
            p_ref[P_X0 + kc] = val
        elif kc < C_TILE0:
            for c in range(R // T):
                bt_ref[kc - B_TILE0, :, c * T:(c + 1) * T] = val[c * T:(c + 1) * T].T.astype(bt_ref.dtype)
        else:
            c_ref[kc - C_TILE0] = val.astype(c_ref.dtype)

    def conv_pair(t):
        col = (N_Z_TILES + t) * LANES
        acc = jnp.dot(xn, w_ref[:, col:col + pair], preferred_element_type=F32)
        put_conv(t, conv_silu(acc[:, :LANES], t))
        put_conv(t + 1, conv_silu(acc[:, LANES:], t + 1))

    for i in range(N_Z_TILES // 2):
        conv_pair(4 * i)
        z_pair(2 * i)
        conv_pair(4 * i + 2)


def _ssd_in_proj(x2d, gain, w_in, w_dt, conv_w, conv_b, dt_bias, a_log):
    m = x2d.shape[0]
    t_major = pl.BlockSpec((ROW_TILE, LANES), lambda i: (i, 0))
    h_major = pl.BlockSpec((LANES, ROW_TILE), lambda i: (0, i))
    return pl.pallas_call(
        _ssd_in_proj_kernel,
        grid=(m // ROW_TILE,),
        in_specs=[
            pl.BlockSpec((ROW_TILE, D_MODEL), lambda i: (i, 0)),
            _resident((1, D_MODEL)),
            _resident(w_in.shape),
            _resident(w_dt.shape),
            _resident((N_CONV_TILES, SSD_CONV_WIDTH, LANES)),
            _resident((N_CONV_TILES, 1, LANES)),
            _resident((1, LANES)),
            _resident((1, LANES)),
        ],
        out_specs=[
            pl.BlockSpec((N_P_TILES, ROW_TILE, LANES), lambda i: (0, i, 0)),
            pl.BlockSpec((SSD_N_GROUPS, ROW_TILE, LANES), lambda i: (0, i, 0)),
            pl.BlockSpec((SSD_N_GROUPS, SSD_D_STATE, ROW_TILE), lambda i: (0, 0, i)),
            t_major, h_major, h_major, h_major,
        ],
        out_shape=[
            jax.ShapeDtypeStruct((N_P_TILES, m, LANES), F32),
            jax.ShapeDtypeStruct((SSD_N_GROUPS, m, LANES), F32),
            jax.ShapeDtypeStruct((SSD_N_GROUPS, SSD_D_STATE, m), F32),
            jax.ShapeDtypeStruct((m, LANES), F32),
            jax.ShapeDtypeStruct((LANES, m), F32),
            jax.ShapeDtypeStruct((LANES, m), F32),
            jax.ShapeDtypeStruct((LANES, m), F32),
        ],
        scratch_shapes=[
            pltpu.VMEM((N_CONV_TILES, HALO_ROWS, LANES), F32),
            pltpu.VMEM((N_STAGE, HALO_ROWS + ROW_TILE, LANES), F32),
        ],
        compiler_params=_params("arbitrary"),
        name="ssd_in_proj",
    )(x2d, gain, w_in, w_dt, conv_w, conv_b, dt_bias, a_log)


def _ssd_kernel(p_ref, c_ref, bt_ref, cum_ref, cumt_ref, dtt_ref, wendt_ref, dskip_ref, nw_ref, o_ref, state_ref):
    T = SSD_CHUNK
    lane = lax.broadcasted_iota(jnp.int32, (T, LANES), 1)
    row = lax.broadcasted_iota(jnp.int32, (T, LANES), 0)
    lane_lo = lane < HEAD_DIM
    causal = row >= lane

    @pl.when(pl.program_id(1) == 0)
    def _():
        state_ref[...] = jnp.zeros_like(state_ref)

    def group(rows, dec, g):
        cm = c_ref[g, rows, :].astype(BF16)
        bt = bt_ref[g, :, rows].astype(BF16)
        cb = jnp.dot(cm, bt, preferred_element_type=F32).astype(BF16)
        gated = []
        for half in range(2):
            kx = 2 * g + half
            x = p_ref[P_X0 + kx, rows, :]
            st = state_ref[kx]
            xb = x.astype(BF16)
            lhs_y = []
            lhs_s = []
            for i in range(2):
                h = 2 * kx + i
                crow = cumt_ref[h:h + 1, rows]
                cbc = jnp.broadcast_to(cum_ref[rows, h:h + 1], (T, LANES))
                decay = jnp.where(causal, jnp.exp2(cbc - crow), 0.0).astype(BF16)
                lhs_y.append(jnp.concatenate(
                    [cb * decay * dtt_ref[h:h + 1, rows].astype(BF16), cm * jnp.exp2(cbc).astype(BF16)],
                    axis=1))
                lhs_s.append(bt * wendt_ref[h:h + 1, rows].astype(BF16))
            yy = jnp.dot(jnp.concatenate(lhs_y, axis=0), jnp.concatenate([xb, st.astype(BF16)], axis=0),
                         preferred_element_type=F32)
            y = jnp.where(lane_lo, yy[:T], yy[T:]) + dskip_ref[kx] * x
            ds = jnp.dot(jnp.concatenate(lhs_s, axis=0), xb, preferred_element_type=F32)
            d0 = jnp.broadcast_to(dec[:, 2 * kx:2 * kx + 1], (1, LANES))
            d1 = jnp.broadcast_to(dec[:, 2 * kx + 1:2 * kx + 2], (1, LANES))
            state_ref[kx] = st * jnp.where(lane_lo[0:1], d0, d1) + jnp.where(
                lane_lo, ds[:SSD_D_STATE], ds[SSD_D_STATE:])
            gated.append(y * p_ref[kx, rows, :])
        ss = (jnp.sum(gated[0] * gated[0], axis=-1, keepdims=True)
              + jnp.sum(gated[1] * gated[1], axis=-1, keepdims=True))
        r = lax.rsqrt(ss * (1.0 / (2 * LANES)) + NORM_EPS)
        for half in range(2):
            kx = 2 * g + half
            o_ref[kx, rows, :] = (gated[half] * r * nw_ref[kx]).astype(o_ref.dtype)

    for ci in range(SSD_CHUNKS_PER_STEP):
        rows = slice(ci * T, (ci + 1) * T)
        dec = jnp.exp2(cum_ref[(ci + 1) * T - 1:(ci + 1) * T, :])
        for g in range(SSD_N_GROUPS):
            group(rows, dec, g)


def _ssd_core(p, c, bt, cum, cumt, dtt, wendt, dskip, norm_w, batch):
    m = p.shape[1]
    R = SSD_CHUNKS_PER_STEP * SSD_CHUNK
    n_steps = m // batch // R
    h_major = pl.BlockSpec((LANES, R), lambda b, c: (0, b * n_steps + c))
    return pl.pallas_call(
        _ssd_kernel,
        grid=(batch, n_steps),
        in_specs=[
            pl.BlockSpec((N_P_TILES, R, LANES), lambda b, c: (0, b * n_steps + c, 0)),
            pl.BlockSpec((SSD_N_GROUPS, R, LANES), lambda b, c: (0, b * n_steps + c, 0)),
            pl.BlockSpec((SSD_N_GROUPS, SSD_D_STATE, R), lambda b, c: (0, 0, b * n_steps + c)),
            pl.BlockSpec((R, LANES), lambda b, c: (b * n_steps + c, 0)),
            h_major, h_major, h_major,
            _resident((N_X_TILES, 1, LANES)),
            _resident((N_X_TILES, 1, LANES)),
        ],
        out_specs=pl.BlockSpec((N_X_TILES, R, LANES), lambda b, c: (0, b * n_steps + c, 0)),
        out_shape=jax.ShapeDtypeStruct((N_X_TILES, m, LANES), BF16),
        scratch_shapes=[pltpu.VMEM((N_X_TILES, SSD_D_STATE, LANES), F32)],
        compiler_params=_params("arbitrary", "arbitrary"),
        name="ssd_core",
    )(p, c, bt, cum, cumt, dtt, wendt, dskip, norm_w)


def _mlp_residual(h, g1_ref, wu_ref, wd_ref, g2_ref):
    u = _rms(h, g1_ref[...]).astype(BF16)
    acc = None
    for c in range(0, D_FF, FF_CHUNK):
        a = jnp.dot(u, wu_ref[:, c:c + FF_CHUNK], preferred_element_type=F32)
        a = jnp.square(jnp.maximum(a, 0.0)).astype(BF16)
        part = jnp.dot(a, wd_ref[c:c + FF_CHUNK, :], preferred_element_type=F32)
        acc = part if acc is None else acc + part
    return h + _rms(acc, g2_ref[...])


def _ssd_tail_kernel(y_ref, wo_ref, h_ref, gp_ref, g1_ref, wu_ref, wd_ref, g2_ref,
                     gq_ref, wq_ref, bq_ref, o_ref, qkv_ref):
    y = jnp.concatenate([y_ref[k] for k in range(N_X_TILES)], axis=1)
    mix = jnp.dot(y, wo_ref[...], preferred_element_type=F32)
    h = _mlp_residual(h_ref[...] + _rms(mix, gp_ref[...]), g1_ref, wu_ref, wd_ref, g2_ref)
    o_ref[...] = h
    xn = _rms(h, gq_ref[...]).astype(BF16)
    qkv = jnp.dot(xn, wq_ref[...], preferred_element_type=F32) + bq_ref[...]
    qkv_ref[...] = qkv.astype(qkv_ref.dtype)


def _attn_tail_kernel(a_ref, wo_ref, bo_ref, h_ref, gp_ref, g1_ref, wu_ref, wd_ref, g2_ref, o_ref):
    mix = jnp.dot(a_ref[...], wo_ref[...], preferred_element_type=F32) + bo_ref[...]
    o_ref[...] = _mlp_residual(h_ref[...] + _rms(mix, gp_ref[...]), g1_ref, wu_ref, wd_ref, g2_ref)


_ROWS = pl.BlockSpec((ROW_TILE, D_MODEL), lambda i: (i, 0))
_GAIN = (1, D_MODEL)


def _ssd_tail(y_tiles, w_out, h, g_post, g1, wu, wd, layer, g2, g_qkv, w_qkv, b_qkv):
    m = h.shape[0]
    return pl.pallas_call(
        _ssd_tail_kernel,
        grid=(m // ROW_TILE,),
        in_specs=[
            pl.BlockSpec((N_X_TILES, ROW_TILE, LANES), lambda i: (0, i, 0)),
            _resident(w_out.shape), _ROWS, _resident(_GAIN),
            _resident(_GAIN), _resident_layer(wu.shape, layer), _resident_layer(wd.shape, layer), _resident(_GAIN),
            _resident(_GAIN), _resident(w_qkv.shape), _resident(b_qkv.shape),
        ],
        out_specs=[_ROWS, pl.BlockSpec((ROW_TILE, ATTN_QKV_DIM), lambda i: (i, 0))],
        out_shape=[jax.ShapeDtypeStruct((m, D_MODEL), F32),
                   jax.ShapeDtypeStruct((m, ATTN_QKV_DIM), BF16)],
        compiler_params=_params("arbitrary"),
        name="ssd_tail_mlp_qkv",
    )(y_tiles, w_out, h, g_post, g1, wu, wd, g2, g_qkv, w_qkv, b_qkv)


def _attn_tail(a, w_o, b_o, h, g_post, g1, wu, wd, layer, g2):
    m = h.shape[0]
    return pl.pallas_call(
        _attn_tail_kernel,
        grid=(m // ROW_TILE,),
        in_specs=[
            pl.BlockSpec((ROW_TILE, ATTN_Q_DIM), lambda i: (i, 0)),
            _resident(w_o.shape), _resident(_GAIN), _ROWS, _resident(_GAIN),
            _resident(_GAIN), _resident_layer(wu.shape, layer), _resident_layer(wd.shape, layer), _resident(_GAIN),
        ],
        out_specs=_ROWS,
        out_shape=jax.ShapeDtypeStruct((m, D_MODEL), F32),
        compiler_params=_params("arbitrary"),
        name="attn_tail_mlp",
    )(a, w_o, b_o, h, g_post, g1, wu, wd, g2)


def _attn_kernel(sink_ref, q_ref, kp_ref, kc_ref, vp_ref, vc_ref, o_ref):
    W = ATTN_WINDOW
    PACK = 16
    top = lax.broadcasted_iota(jnp.int32, (PACK, 2 * LANES), 0) == 0
    zero = jnp.zeros((), BF16)

    def window(prev_ref, cur_ref, blk):
        prev = prev_ref[...] if blk == 0 else cur_ref[(blk - 1) * W:blk * W, :]
        return jnp.concatenate(
            [jnp.where(top, zero, prev[:PACK]), prev[PACK:], cur_ref[blk * W:(blk + 1) * W, :]], axis=0)

    qrow = lax.broadcasted_iota(jnp.int32, (2 * W, 2 * W), 0)
    kpos = lax.broadcasted_iota(jnp.int32, (2 * W, 2 * W), 1)
    qpos = jnp.where(qrow >= W, qrow - W, qrow) + W
    rel = qpos - kpos
    band = (rel >= 0) & (rel < W)
    lane = lax.broadcasted_iota(jnp.int32, (W, LANES), 1)
    lane_lo = lane < HEAD_DIM
    q_extra = jnp.concatenate([jnp.where(lane < 3, 1.0, 0.0),
                               jnp.where((lane >= 3) & (lane < 6), 1.0, 0.0)], axis=0).astype(BF16)
    lane1 = lax.broadcasted_iota(jnp.int32, (1, LANES), 1)
    is_piece0 = (lane1 == 0) | (lane1 == 3)
    is_piece1 = (lane1 == 1) | (lane1 == 4)
    k_extras = []
    for t in range(ATTN_Q_DIM // LANES):
        sv = sink_ref[t]
        s_hi = sv.astype(BF16).astype(F32)
        s_mid = (sv - s_hi).astype(BF16).astype(F32)
        s_lo = (sv - s_hi - s_mid).astype(BF16).astype(F32)
        srow = jnp.where(is_piece0, s_hi, jnp.where(is_piece1, s_mid, s_lo))
        k_extras.append(jnp.concatenate(
            [jnp.where(top[:, :LANES], jnp.broadcast_to(srow, (PACK, LANES)), 0.0).astype(BF16),
             jnp.zeros((2 * W - PACK, LANES), BF16)], axis=0))

    for blk in range(ATTN_BLOCKS_PER_STEP):
        rows = slice(blk * W, (blk + 1) * W)
        kcat = window(kp_ref, kc_ref, blk)
        vcat = window(vp_ref, vc_ref, blk)
        if blk == 0:
            kmin = jnp.where(pl.program_id(1) == 0, W, 0)
            valid = (band & (kpos >= kmin)) | (kpos == 0)
        else:
            valid = band | (kpos == 0)
        bias = jnp.where(valid, 0.0, -jnp.inf)
        for t in range(ATTN_Q_DIM // LANES):
            pair = t // ATTN_REP
            qt = q_ref[rows, t * LANES:(t + 1) * LANES] * jnp.asarray(HEAD_DIM ** -0.5, BF16)
            qs = jnp.concatenate([jnp.where(lane_lo, qt, zero), jnp.where(lane_lo, zero, qt)], axis=0)
            kt = jnp.concatenate([kcat[:, pair * LANES:(pair + 1) * LANES], k_extras[t]], axis=1)
            vt = vcat[:, pair * LANES:(pair + 1) * LANES]
            s = lax.dot_general(jnp.concatenate([qs, q_extra], axis=1), kt, (((1,), (1,)), ((), ())),
                                preferred_element_type=F32) + bias
            mx = jnp.max(s, axis=-1, keepdims=True)
            e = jnp.exp(s - mx)
            den = jnp.sum(e, axis=-1, keepdims=True)
            pv = jnp.dot(e.astype(BF16), vt, preferred_element_type=F32) * (1.0 / den)
            o_ref[rows, t * LANES:(t + 1) * LANES] = jnp.where(lane_lo, pv[:W], pv[W:]).astype(o_ref.dtype)


def _attention(qkv, sinks, batch):
    m = qkv.shape[0]
    W = ATTN_WINDOW
    R = ATTN_BLOCKS_PER_STEP * W
    nb = m // batch // W
    n_steps = nb // ATTN_BLOCKS_PER_STEP
    k_col = ATTN_Q_DIM // ATTN_KV_DIM
    v_col = k_col + 1

    def cur(b, i):
        return b * n_steps + i

    def prev(b, i):
        return jnp.maximum(b * nb + ATTN_BLOCKS_PER_STEP * i - 1, 0)

    return pl.pallas_call(
        _attn_kernel,
        grid=(batch, n_steps),
        in_specs=[
            _resident(sinks.shape),
            pl.BlockSpec((R, ATTN_Q_DIM), lambda b, i: (cur(b, i), 0)),
            pl.BlockSpec((W, ATTN_KV_DIM), lambda b, i: (prev(b, i), k_col)),
            pl.BlockSpec((R, ATTN_KV_DIM), lambda b, i: (cur(b, i), k_col)),
            pl.BlockSpec((W, ATTN_KV_DIM), lambda b, i: (prev(b, i), v_col)),
            pl.BlockSpec((R, ATTN_KV_DIM), lambda b, i: (cur(b, i), v_col)),
        ],
        out_specs=pl.BlockSpec((R, ATTN_Q_DIM), lambda b, i: (cur(b, i), 0)),
        out_shape=jax.ShapeDtypeStruct((m, ATTN_Q_DIM), BF16),
        compiler_params=_params("arbitrary", "arbitrary"),
        name="swa_sink_attention",
    )(sinks, qkv, qkv, qkv, qkv, qkv)


def _permute_q_heads(v, axis):
    inner = v.shape[axis] // ATTN_N_Q_HEADS
    shape = v.shape[:axis] + (ATTN_N_KV_HEADS // 2, 2, ATTN_REP, inner) + v.shape[axis + 1:]
    order = list(range(len(shape)))
    order[axis + 1], order[axis + 2] = axis + 2, axis + 1
    return v.reshape(shape).transpose(order).reshape(v.shape)


def _lane_tiles(v, n_tiles):
    return v.astype(F32).reshape(n_tiles, 1, LANES)


def kernel(x, ssd_w_in, ssd_conv_w, ssd_conv_b, ssd_dt_bias, ssd_a_log, ssd_d, ssd_norm_w, ssd_w_out, attn_w_qkv, attn_b_qkv, attn_sinks, attn_w_o, attn_b_o, mlp_w_up, mlp_w_down, mix_pre_norm, mix_post_norm, ffn_pre_norm, ffn_post_norm):
    batch, seq, d = x.shape
    m = batch * seq
    h = x.reshape(m, d)
    row = lambda v: v.astype(F32).reshape(1, -1)

    w_in = ssd_w_in[0].astype(BF16)
    w_dt = jnp.pad(ssd_w_in[0][:, SSD_IN_DIM - SSD_N_HEADS:], ((0, 0), (0, LANES - SSD_N_HEADS))).astype(BF16)
    conv_w = ssd_conv_w[0].astype(F32).reshape(SSD_CONV_WIDTH, N_CONV_TILES, LANES).transpose(1, 0, 2)
    pad_heads = lambda v: jnp.pad(v.astype(F32), (0, LANES - SSD_N_HEADS)).reshape(1, LANES)
    assert seq == SEQ and seq % ROW_TILE == 0
    p, c, bt, cum, cumt, dtt, wendt = _ssd_in_proj(
        h, row(mix_pre_norm[0]), w_in, w_dt, conv_w, _lane_tiles(ssd_conv_b[0], N_CONV_TILES),
        pad_heads(ssd_dt_bias[0]), pad_heads(ssd_a_log[0]))
    y = _ssd_core(
        p, c, bt, cum, cumt, dtt, wendt, _lane_tiles(jnp.repeat(ssd_d[0], HEAD_DIM), N_X_TILES),
        _lane_tiles(ssd_norm_w[0], N_X_TILES), batch)
    wq, wkv = attn_w_qkv[0][:, :ATTN_Q_DIM], attn_w_qkv[0][:, ATTN_Q_DIM:]
    bq, bkv = attn_b_qkv[0][:ATTN_Q_DIM], attn_b_qkv[0][ATTN_Q_DIM:]
    w_qkv = jnp.concatenate([_permute_q_heads(wq, 1), wkv], axis=1).astype(BF16)
    b_qkv = jnp.concatenate([_permute_q_heads(bq, 0), bkv]).astype(F32).reshape(1, -1)
    w_up, w_down = mlp_w_up.astype(BF16), mlp_w_down.astype(BF16)
    h, qkv = _ssd_tail(
        y, ssd_w_out[0].astype(BF16), h, row(mix_post_norm[0]),
        row(ffn_pre_norm[0]), w_up, w_down, 0, row(ffn_post_norm[0]),
        row(mix_pre_norm[1]), w_qkv, b_qkv)

    sink_pairs = _permute_q_heads(attn_sinks[0], 0).astype(F32).reshape(ATTN_Q_DIM // LANES, 2, 1)
    sink_rows = jnp.pad(jnp.broadcast_to(sink_pairs, sink_pairs.shape[:2] + (3,)).reshape(-1, 1, 6),
                        ((0, 0), (0, 0), (0, LANES - 6)))
    a = _attention(qkv, sink_rows, batch)
    h = _attn_tail(
        a, _permute_q_heads(attn_w_o[0], 0).astype(BF16), row(attn_b_o[0]), h, row(mix_post_norm[1]),
        row(ffn_pre_norm[1]), w_up, w_down, 1, row(ffn_post_norm[1]))
    return h.reshape(batch, seq, d)
```

```python
import functools

import jax
import jax.numpy as jnp
from jax import lax
from jax.experimental import pallas as pl
from jax.experimental.pallas import tpu as pltpu

F32 = jnp.float32
BF16 = jnp.bfloat16

D_MODEL = 1024
SEQ = 4096
NORM_EPS = 1e-6
LOG2_E = 1.4426950408889634

LANES = 128
HEAD_DIM = 64

SSD_D_INNER = 2048
SSD_N_HEADS = 32
SSD_N_GROUPS = 8
SSD_D_STATE = 128
SSD_CONV_WIDTH = 4
SSD_CHUNK = 128
SSD_CONV_DIM = SSD_D_INNER + 2 * SSD_N_GROUPS * SSD_D_STATE
SSD_IN_DIM = SSD_D_INNER + SSD_CONV_DIM + SSD_N_HEADS
N_Z_TILES = SSD_D_INNER // LANES
N_X_TILES = SSD_D_INNER // LANES
N_CONV_TILES = SSD_CONV_DIM // LANES
N_IN_TILES = N_Z_TILES + N_CONV_TILES + 1
B_TILE0 = N_X_TILES
C_TILE0 = N_X_TILES + SSD_N_GROUPS
HALO_ROWS = 8
SSD_CHUNKS_PER_STEP = 4
N_STAGE = 4
P_X0 = N_Z_TILES
N_P_TILES = N_Z_TILES + N_X_TILES

ATTN_N_Q_HEADS = 16
ATTN_N_KV_HEADS = 4
ATTN_REP = 4
ATTN_WINDOW = 128
ATTN_BLOCKS_PER_STEP = 8
ATTN_Q_DIM = ATTN_N_Q_HEADS * HEAD_DIM
ATTN_KV_DIM = ATTN_N_KV_HEADS * HEAD_DIM
ATTN_QKV_DIM = ATTN_Q_DIM + 2 * ATTN_KV_DIM

D_FF = 4096
FF_CHUNK = 1024

ROW_TILE = 512
VMEM_LIMIT = 56 * 1024 * 1024


def _rms(x, gain):
    ms = jnp.mean(x * x, axis=-1, keepdims=True)
    return x * lax.rsqrt(ms + NORM_EPS) * gain


def _sigmoid(x):
    return 1.0 / (1.0 + jnp.exp2(x * (-LOG2_E)))


def _resident(shape):
    nd = len(shape)
    return pl.BlockSpec(shape, lambda *_: (0,) * nd, pipeline_mode=pl.Buffered(1))


def _resident_layer(stacked_shape, layer):
    return pl.BlockSpec((None,) + tuple(stacked_shape[1:]), lambda *_: (layer, 0, 0),
                        pipeline_mode=pl.Buffered(1))


def _params(*sem):
    return pltpu.CompilerParams(dimension_semantics=sem, vmem_limit_bytes=VMEM_LIMIT)


def _ssd_in_proj_kernel(x_ref, g_ref, w_ref, wdt_ref, cw_ref, cb_ref, dtb_ref, alog_ref,
                        p_ref, c_ref, bt_ref, cum_ref, cumt_ref, dtt_ref, wendt_ref, halo_ref, stage_ref):
    R = ROW_TILE
    T = SSD_CHUNK

    @pl.when(pl.program_id(0) % (SEQ // ROW_TILE) == 0)
    def _():
        halo_ref[...] = jnp.zeros_like(halo_ref)

    xn = _rms(x_ref[...], g_ref[...]).astype(BF16)
    pair = 2 * LANES

    dt_in = jnp.dot(xn, wdt_ref[...], preferred_element_type=F32) + dtb_ref[...]
    dt = jnp.maximum(dt_in, 0.0) + jnp.log1p(jnp.exp(-jnp.abs(dt_in)))
    aq = dt * (-jnp.exp(alog_ref[...]))
    tri = (lax.broadcasted_iota(jnp.int32, (T, T), 0)
           >= lax.broadcasted_iota(jnp.int32, (T, T), 1)).astype(BF16)
    aq_hi = aq.astype(BF16)
    rem = aq - aq_hi.astype(F32)
    aq_mid = rem.astype(BF16)
    aq_lo = (rem - aq_mid.astype(F32)).astype(BF16)
    for c in range(R // T):
        rows = slice(c * T, (c + 1) * T)
        cum = (jnp.dot(tri, aq_hi[rows], preferred_element_type=F32)
               + jnp.dot(tri, aq_mid[rows], preferred_element_type=F32)
               + jnp.dot(tri, aq_lo[rows], preferred_element_type=F32))
        cum_last = cum[T - 1:T, :]
        cum2 = cum * LOG2_E
        cum_ref[rows, :] = cum2
        cumt_ref[:, rows] = cum2.T
        dtt_ref[:, rows] = dt[rows].T
        wendt_ref[:, rows] = (dt[rows] * jnp.exp(cum_last - cum)).T

    def z_pair(t):
        acc = jnp.dot(xn, w_ref[:, t * LANES:t * LANES + pair], preferred_element_type=F32)
        gate = acc * _sigmoid(acc)
        p_ref[t] = gate[:, :LANES]
        p_ref[t + 1] = gate[:, LANES:]

    def conv_silu(raw, kc):
        w = cw_ref[kc]
        slot = kc % N_STAGE
        stage_ref[slot, 0:HALO_ROWS, :] = halo_ref[kc]
        stage_ref[slot, HALO_ROWS:, :] = raw
        acc = cb_ref[kc] + w[SSD_CONV_WIDTH - 1:SSD_CONV_WIDTH] * raw
        for d in range(1, SSD_CONV_WIDTH):
            shifted = stage_ref[slot, HALO_ROWS - d:HALO_ROWS - d + R, :]
            acc = acc + w[SSD_CONV_WIDTH - 1 - d:SSD_CONV_WIDTH - d] * shifted
        halo_ref[kc] = raw[R - HALO_ROWS:]
        return acc * _sigmoid(acc)

    def put_conv(kc, val):
        if kc < B_TILE0:
            p_ref[P_X0 + kc] = val
        elif kc < C_TILE0:
            for c in range(R // T):
                bt_ref[kc - B_TILE0, :, c * T:(c + 1) * T] = val[c * T:(c + 1) * T].T.astype(bt_ref.dtype)
        else:
            c_ref[kc - C_TILE0] = val.astype(c_ref.dtype)

    def conv_pair(t):
        col = (N_Z_TILES + t) * LANES
        acc = jnp.dot(xn, w_ref[:, col:col + pair], preferred_element_type=F32)
        put_conv(t, conv_silu(acc[:, :LANES], t))
        put_conv(t + 1, conv_silu(acc[:, LANES:], t + 1))

    for i in range(N_Z_TILES // 2):
        conv_pair(4 * i)
        z_pair(2 * i)
        conv_pair(4 * i + 2)


def _ssd_in_proj(x2d, gain, w_in, w_dt, conv_w, conv_b, dt_bias, a_log):
    m = x2d.shape[0]
    t_major = pl.BlockSpec((ROW_TILE, LANES), lambda i: (i, 0))
    h_major = pl.BlockSpec((LANES, ROW_TILE), lambda i: (0, i))
    return pl.pallas_call(
        _ssd_in_proj_kernel,
        grid=(m // ROW_TILE,),
        in_specs=[
            pl.BlockSpec((ROW_TILE, D_MODEL), lambda i: (i, 0)),
            _resident((1, D_MODEL)),
            _resident(w_in.shape),
            _resident(w_dt.shape),
            _resident((N_CONV_TILES, SSD_CONV_WIDTH, LANES)),
            _resident((N_CONV_TILES, 1, LANES)),
            _resident((1, LANES)),
            _resident((1, LANES)),
        ],
        out_specs=[
            pl.BlockSpec((N_P_TILES, ROW_TILE, LANES), lambda i: (0, i, 0)),
            pl.BlockSpec((SSD_N_GROUPS, ROW_TILE, LANES), lambda i: (0, i, 0)),
            pl.BlockSpec((SSD_N_GROUPS, SSD_D_STATE, ROW_TILE), lambda i: (0, 0, i)),
            t_major, h_major, h_major, h_major,
        ],
        out_shape=[
            jax.ShapeDtypeStruct((N_P_TILES, m, LANES), F32),
            jax.ShapeDtypeStruct((SSD_N_GROUPS, m, LANES), F32),
            jax.ShapeDtypeStruct((SSD_N_GROUPS, SSD_D_STATE, m), F32),
            jax.ShapeDtypeStruct((m, LANES), F32),
            jax.ShapeDtypeStruct((LANES, m), F32),
            jax.ShapeDtypeStruct((LANES, m), F32),
            jax.ShapeDtypeStruct((LANES, m), F32),
        ],
        scratch_shapes=[
            pltpu.VMEM((N_CONV_TILES, HALO_ROWS, LANES), F32),
            pltpu.VMEM((N_STAGE, HALO_ROWS + ROW_TILE, LANES), F32),
        ],
        compiler_params=_params("arbitrary"),
        name="ssd_in_proj",
    )(x2d, gain, w_in, w_dt, conv_w, conv_b, dt_bias, a_log)


def _ssd_kernel(p_ref, c_ref, bt_ref, cum_ref, cumt_ref, dtt_ref, wendt_ref, dskip_ref, nw_ref, o_ref, state_ref):
    T = SSD_CHUNK
    lane = lax.broadcasted_iota(jnp.int32, (T, LANES), 1)
    row = lax.broadcasted_iota(jnp.int32, (T, LANES), 0)
    lane_lo = lane < HEAD_DIM
    causal = row >= lane

    @pl.when(pl.program_id(1) == 0)
    def _():
        state_ref[...] = jnp.zeros_like(state_ref)

    def group(rows, dec, g):
        cm = c_ref[g, rows, :].astype(BF16)
        bt = bt_ref[g, :, rows].astype(BF16)
        cb = jnp.dot(cm, bt, preferred_element_type=F32).astype(BF16)
        gated = []
        for half in range(2):
            kx = 2 * g + half
            x = p_ref[P_X0 + kx, rows, :]
            st = state_ref[kx]
            xb = x.astype(BF16)
            lhs_y = []
            lhs_s = []
            for i in range(2):
                h = 2 * kx + i
                crow = cumt_ref[h:h + 1, rows]
                cbc = jnp.broadcast_to(cum_ref[rows, h:h + 1], (T, LANES))
                decay = jnp.where(causal, jnp.exp2(cbc - crow), 0.0).astype(BF16)
                lhs_y.append(jnp.concatenate(
                    [cb * decay * dtt_ref[h:h + 1, rows].astype(BF16), cm * jnp.exp2(cbc).astype(BF16)],
                    axis=1))
                lhs_s.append(bt * wendt_ref[h:h + 1, rows].astype(BF16))
            yy = jnp.dot(jnp.concatenate(lhs_y, axis=0), jnp.concatenate([xb, st.astype(BF16)], axis=0),
                         preferred_element_type=F32)
            y = jnp.where(lane_lo, yy[:T], yy[T:]) + dskip_ref[kx] * x
            ds = jnp.dot(jnp.concatenate(lhs_s, axis=0), xb, preferred_element_type=F32)
            d0 = jnp.broadcast_to(dec[:, 2 * kx:2 * kx + 1], (1, LANES))
            d1 = jnp.broadcast_to(dec[:, 2 * kx + 1:2 * kx + 2], (1, LANES))
            state_ref[kx] = st * jnp.where(lane_lo[0:1], d0, d1) + jnp.where(
                lane_lo, ds[:SSD_D_STATE], ds[SSD_D_STATE:])
            gated.append(y * p_ref[kx, rows, :])
        ss = (jnp.sum(gated[0] * gated[0], axis=-1, keepdims=True)
              + jnp.sum(gated[1] * gated[1], axis=-1, keepdims=True))
        r = lax.rsqrt(ss * (1.0 / (2 * LANES)) + NORM_EPS)
        for half in range(2):
            kx = 2 * g + half
            o_ref[kx, rows, :] = (gated[half] * r * nw_ref[kx]).astype(o_ref.dtype)

    for ci in range(SSD_CHUNKS_PER_STEP):
        rows = slice(ci * T, (ci + 1) * T)
        dec = jnp.exp2(cum_ref[(ci + 1) * T - 1:(ci + 1) * T, :])
        for g in range(SSD_N_GROUPS):
            group(rows, dec, g)


def _ssd_core(p, c, bt, cum, cumt, dtt, wendt, dskip, norm_w, batch):
    m = p.shape[1]
    R = SSD_CHUNKS_PER_STEP * SSD_CHUNK
    n_steps = m // batch // R
    h_major = pl.BlockSpec((LANES, R), lambda b, c: (0, b * n_steps + c))
    return pl.pallas_call(
        _ssd_kernel,
        grid=(batch, n_steps),
        in_specs=[
            pl.BlockSpec((N_P_TILES, R, LANES), lambda b, c: (0, b * n_steps + c, 0)),
            pl.BlockSpec((SSD_N_GROUPS, R, LANES), lambda b, c: (0, b * n_steps + c, 0)),
            pl.BlockSpec((SSD_N_GROUPS, SSD_D_STATE, R), lambda b, c: (0, 0, b * n_steps + c)),
            pl.BlockSpec((R, LANES), lambda b, c: (b * n_steps + c, 0)),
            h_major, h_major, h_major,
            _resident((N_X_TILES, 1, LANES)),
            _resident((N_X_TILES, 1, LANES)),
        ],
        out_specs=pl.BlockSpec((N_X_TILES, R, LANES), lambda b, c: (0, b * n_steps + c, 0)),
        out_shape=jax.ShapeDtypeStruct((N_X_TILES, m, LANES), BF16),
        scratch_shapes=[pltpu.VMEM((N_X_TILES, SSD_D_STATE, LANES), F32)],
        compiler_params=_params("arbitrary", "arbitrary"),
        name="ssd_core",
    )(p, c, bt, cum, cumt, dtt, wendt, dskip, norm_w)


def _mlp_residual(h, g1_ref, wu_ref, wd_ref, g2_ref):
    u = _rms(h, g1_ref[...]).astype(BF16)
    acc = None
    for c in range(0, D_FF, FF_CHUNK):
        a = jnp.dot(u, wu_ref[:, c:c + FF_CHUNK], preferred_element_type=F32)
        a = jnp.square(jnp.maximum(a, 0.0)).astype(BF16)
        part = jnp.dot(a, wd_ref[c:c + FF_CHUNK, :], preferred_element_type=F32)
        acc = part if acc is None else acc + part
    return h + _rms(acc, g2_ref[...])


def _ssd_tail_kernel(y_ref, wo_ref, h_ref, gp_ref, g1_ref, wu_ref, wd_ref, g2_ref,
                     gq_ref, wq_ref, bq_ref, o_ref, qkv_ref):
    y = jnp.concatenate([y_ref[k] for k in range(N_X_TILES)], axis=1)
    mix = jnp.dot(y, wo_ref[...], preferred_element_type=F32)
    h = _mlp_residual(h_ref[...] + _rms(mix, gp_ref[...]), g1_ref, wu_ref, wd_ref, g2_ref)
    o_ref[...] = h
    xn = _rms(h, gq_ref[...]).astype(BF16)
    qkv = jnp.dot(xn, wq_ref[...], preferred_element_type=F32) + bq_ref[...]
    qkv_ref[...] = qkv.astype(qkv_ref.dtype)


def _attn_tail_kernel(a_ref, wo_ref, bo_ref, h_ref, gp_ref, g1_ref, wu_ref, wd_ref, g2_ref, o_ref):
    mix = jnp.dot(a_ref[...], wo_ref[...], preferred_element_type=F32) + bo_ref[...]
    o_ref[...] = _mlp_residual(h_ref[...] + _rms(mix, gp_ref[...]), g1_ref, wu_ref, wd_ref, g2_ref)


_ROWS = pl.BlockSpec((ROW_TILE, D_MODEL), lambda i: (i, 0))
_GAIN = (1, D_MODEL)


def _ssd_tail(y_tiles, w_out, h, g_post, g1, wu, wd, layer, g2, g_qkv, w_qkv, b_qkv):
    m = h.shape[0]
    return pl.pallas_call(
        _ssd_tail_kernel,
        grid=(m // ROW_TILE,),
        in_specs=[
            pl.BlockSpec((N_X_TILES, ROW_TILE, LANES), lambda i: (0, i, 0)),
            _resident(w_out.shape), _ROWS, _resident(_GAIN),
            _resident(_GAIN), _resident_layer(wu.shape, layer), _resident_layer(wd.shape, layer), _resident(_GAIN),
            _resident(_GAIN), _resident(w_qkv.shape), _resident(b_qkv.shape),
        ],
        out_specs=[_ROWS, pl.BlockSpec((ROW_TILE, ATTN_QKV_DIM), lambda i: (i, 0))],
        out_shape=[jax.ShapeDtypeStruct((m, D_MODEL), F32),
                   jax.ShapeDtypeStruct((m, ATTN_QKV_DIM), BF16)],
        compiler_params=_params("arbitrary"),
        name="ssd_tail_mlp_qkv",
    )(y_tiles, w_out, h, g_post, g1, wu, wd, g2, g_qkv, w_qkv, b_qkv)


def _attn_tail(a, w_o, b_o, h, g_post, g1, wu, wd, layer, g2):
    m = h.shape[0]
    return pl.pallas_call(
        _attn_tail_kernel,
        grid=(m // ROW_TILE,),
        in_specs=[
            pl.BlockSpec((ROW_TILE, ATTN_Q_DIM), lambda i: (i, 0)),
            _resident(w_o.shape), _resident(_GAIN), _ROWS, _resident(_GAIN),
            _resident(_GAIN), _resident_layer(wu.shape, layer), _resident_layer(wd.shape, layer), _resident(_GAIN),
        ],
        out_specs=_ROWS,
        out_shape=jax.ShapeDtypeStruct((m, D_MODEL), F32),
        compiler_params=_params("arbitrary"),
        name="attn_tail_mlp",
    )(a, w_o, b_o, h, g_post, g1, wu, wd, g2)


def _attn_kernel(sink_ref, q_ref, kp_ref, kc_ref, vp_ref, vc_ref, o_ref):
    W = ATTN_WINDOW
    PACK = 16
    top = lax.broadcasted_iota(jnp.int32, (PACK, 2 * LANES), 0) == 0
    zero = jnp.zeros((), BF16)

    def window(prev_ref, cur_ref, blk):
        prev = prev_ref[...] if blk == 0 else cur_ref[(blk - 1) * W:blk * W, :]
        return jnp.concatenate(
            [jnp.where(top, zero, prev[:PACK]), prev[PACK:], cur_ref[blk * W:(blk + 1) * W, :]], axis=0)

    qrow = lax.broadcasted_iota(jnp.int32, (2 * W, 2 * W), 0)
    kpos = lax.broadcasted_iota(jnp.int32, (2 * W, 2 * W), 1)
    qpos = jnp.where(qrow >= W, qrow - W, qrow) + W
    rel = qpos - kpos
    band = (rel >= 0) & (rel < W)
    lane = lax.broadcasted_iota(jnp.int32, (W, LANES), 1)
    lane_lo = lane < HEAD_DIM
    q_extra = jnp.concatenate([jnp.where(lane < 3, 1.0, 0.0),
                               jnp.where((lane >= 3) & (lane < 6), 1.0, 0.0)], axis=0).astype(BF16)
    lane1 = lax.broadcasted_iota(jnp.int32, (1, LANES), 1)
    is_piece0 = (lane1 == 0) | (lane1 == 3)
    is_piece1 = (lane1 == 1) | (lane1 == 4)
    k_extras = []
    for t in range(ATTN_Q_DIM // LANES):
        sv = sink_ref[t]
        s_hi = sv.astype(BF16).astype(F32)
        s_mid = (sv - s_hi).astype(BF16).astype(F32)
        s_lo = (sv - s_hi - s_mid).astype(BF16).astype(F32)
        srow = jnp.where(is_piece0, s_hi, jnp.where(is_piece1, s_mid, s_lo))
        k_extras.append(jnp.concatenate(
            [jnp.where(top[:, :LANES], jnp.broadcast_to(srow, (PACK, LANES)), 0.0).astype(BF16),
             jnp.zeros((2 * W - PACK, LANES), BF16)], axis=0))

    ones_keys = jnp.ones((2 * W, LANES), BF16)
    for blk in range(ATTN_BLOCKS_PER_STEP):
        rows = slice(blk * W, (blk + 1) * W)
        kcat = window(kp_ref, kc_ref, blk)
        vcat = window(vp_ref, vc_ref, blk)
        if blk == 0:
            kmin = jnp.where(pl.program_id(1) == 0, W, 0)
            valid = (band & (kpos >= kmin)) | (kpos == 0)
        else:
            valid = band | (kpos == 0)
        bias = jnp.where(valid, 0.0, -jnp.inf)
        for t in range(ATTN_Q_DIM // LANES):
            pair = t // ATTN_REP
            qt = q_ref[rows, t * LANES:(t + 1) * LANES] * jnp.asarray(HEAD_DIM ** -0.5, BF16)
            qs = jnp.concatenate([jnp.where(lane_lo, qt, zero), jnp.where(lane_lo, zero, qt)], axis=0)
            kt = jnp.concatenate([kcat[:, pair * LANES:(pair + 1) * LANES], k_extras[t]], axis=1)
            vt = jnp.concatenate([vcat[:, pair * LANES:(pair + 1) * LANES], ones_keys], axis=1)
            s = lax.dot_general(jnp.concatenate([qs, q_extra], axis=1), kt, (((1,), (1,)), ((), ())),
                                preferred_element_type=F32) + bias
            mx = jnp.max(s, axis=-1, keepdims=True)
            e = jnp.exp((s - mx).astype(BF16))
            pv = jnp.dot(e, vt, preferred_element_type=F32)
            pv = pv[:, :LANES] * (1.0 / pv[:, LANES:])
            o_ref[rows, t * LANES:(t + 1) * LANES] = jnp.where(lane_lo, pv[:W], pv[W:]).astype(o_ref.dtype)


def _attention(qkv, sinks, batch):
    m = qkv.shape[0]
    W = ATTN_WINDOW
    R = ATTN_BLOCKS_PER_STEP * W
    nb = m // batch // W
    n_steps = nb // ATTN_BLOCKS_PER_STEP
    k_col = ATTN_Q_DIM // ATTN_KV_DIM
    v_col = k_col + 1

    def cur(b, i):
        return b * n_steps + i

    def prev(b, i):
        return jnp.maximum(b * nb + ATTN_BLOCKS_PER_STEP * i - 1, 0)

    return pl.pallas_call(
        _attn_kernel,
        grid=(batch, n_steps),
        in_specs=[
            _resident(sinks.shape),
            pl.BlockSpec((R, ATTN_Q_DIM), lambda b, i: (cur(b, i), 0)),
            pl.BlockSpec((W, ATTN_KV_DIM), lambda b, i: (prev(b, i), k_col)),
            pl.BlockSpec((R, ATTN_KV_DIM), lambda b, i: (cur(b, i), k_col)),
            pl.BlockSpec((W, ATTN_KV_DIM), lambda b, i: (prev(b, i), v_col)),
            pl.BlockSpec((R, ATTN_KV_DIM), lambda b, i: (cur(b, i), v_col)),
        ],
        out_specs=pl.BlockSpec((R, ATTN_Q_DIM), lambda b, i: (cur(b, i), 0)),
        out_shape=jax.ShapeDtypeStruct((m, ATTN_Q_DIM), BF16),
        compiler_params=_params("arbitrary", "arbitrary"),
        name="swa_sink_attention",
    )(sinks, qkv, qkv, qkv, qkv, qkv)


def _permute_q_heads(v, axis):
    inner = v.shape[axis] // ATTN_N_Q_HEADS
    shape = v.shape[:axis] + (ATTN_N_KV_HEADS // 2, 2, ATTN_REP, inner) + v.shape[axis + 1:]
    order = list(range(len(shape)))
    order[axis + 1], order[axis + 2] = axis + 2, axis + 1
    return v.reshape(shape).transpose(order).reshape(v.shape)


def _lane_tiles(v, n_tiles):
    return v.astype(F32).reshape(n_tiles, 1, LANES)


def kernel(x, ssd_w_in, ssd_conv_w, ssd_conv_b, ssd_dt_bias, ssd_a_log, ssd_d, ssd_norm_w, ssd_w_out, attn_w_qkv, attn_b_qkv, attn_sinks, attn_w_o, attn_b_o, mlp_w_up, mlp_w_down, mix_pre_norm, mix_post_norm, ffn_pre_norm, ffn_post_norm):
    batch, seq, d = x.shape
    m = batch * seq
    h = x.reshape(m, d)
    row = lambda v: v.astype(F32).reshape(1, -1)

    w_in = ssd_w_in[0].astype(BF16)
    w_dt = jnp.pad(ssd_w_in[0][:, SSD_IN_DIM - SSD_N_HEADS:], ((0, 0), (0, LANES - SSD_N_HEADS))).astype(BF16)
    conv_w = ssd_conv_w[0].astype(F32).reshape(SSD_CONV_WIDTH, N_CONV_TILES, LANES).transpose(1, 0, 2)
    pad_heads = lambda v: jnp.pad(v.astype(F32), (0, LANES - SSD_N_HEADS)).reshape(1, LANES)
    assert seq == SEQ and seq % ROW_TILE == 0
    p, c, bt, cum, cumt, dtt, wendt = _ssd_in_proj(
        h, row(mix_pre_norm[0]), w_in, w_dt, conv_w, _lane_tiles(ssd_conv_b[0], N_CONV_TILES),
        pad_heads(ssd_dt_bias[0]), pad_heads(ssd_a_log[0]))
    y = _ssd_core(
        p, c, bt, cum, cumt, dtt, wendt, _lane_tiles(jnp.repeat(ssd_d[0], HEAD_DIM), N_X_TILES),
        _lane_tiles(ssd_norm_w[0], N_X_TILES), batch)
    wq, wkv = attn_w_qkv[0][:, :ATTN_Q_DIM], attn_w_qkv[0][:, ATTN_Q_DIM:]
    bq, bkv = attn_b_qkv[0][:ATTN_Q_DIM], attn_b_qkv[0][ATTN_Q_DIM:]
    w_qkv = jnp.concatenate([_permute_q_heads(wq, 1), wkv], axis=1).astype(BF16)
    b_qkv = jnp.concatenate([_permute_q_heads(bq, 0), bkv]).astype(F32).reshape(1, -1)
    w_up, w_down = mlp_w_up.astype(BF16), mlp_w_down.astype(BF16)
    h, qkv = _ssd_tail(
        y, ssd_w_out[0].astype(BF16), h, row(mix_post_norm[0]),
        row(ffn_pre_norm[0]), w_up, w_down, 0, row(ffn_post_norm[0]),
        row(mix_pre_norm[1]), w_qkv, b_qkv)

    sink_pairs = _permute_q_heads(attn_sinks[0], 0).astype(F32).reshape(ATTN_Q_DIM // LANES, 2, 1)
    sink_rows = jnp.pad(jnp.broadcast_to(sink_pairs, sink_pairs.shape[:2] + (3,)).reshape(-1, 1, 6),
                        ((0, 0), (0, 0), (0, LANES - 6)))
    a = _attention(qkv, sink_rows, batch)
    h = _attn_tail(
        a, _permute_q_heads(attn_w_o[0], 0).astype(BF16), row(attn_b_o[0]), h, row(mix_post_norm[1]),
        row(ffn_pre_norm[1]), w_up, w_down, 1, row(ffn_post_norm[1]))
    return h.reshape(batch, seq, d)
```

```python
import functools

import jax
import jax.numpy as jnp
from jax import lax
from jax.experimental import pallas as pl
from jax.experimental.pallas import tpu as pltpu

F32 = jnp.float32
BF16 = jnp.bfloat16

D_MODEL = 1024
SEQ = 4096
NORM_EPS = 1e-6
LOG2_E = 1.4426950408889634

LANES = 128
HEAD_DIM = 64

SSD_D_INNER = 2048
SSD_N_HEADS = 32
SSD_N_GROUPS = 8
SSD_D_STATE = 128
SSD_CONV_WIDTH = 4
SSD_CHUNK = 128
SSD_CONV_DIM = SSD_D_INNER + 2 * SSD_N_GROUPS * SSD_D_STATE
SSD_IN_DIM = SSD_D_INNER + SSD_CONV_DIM + SSD_N_HEADS
N_Z_TILES = SSD_D_INNER // LANES
N_X_TILES = SSD_D_INNER // LANES
N_CONV_TILES = SSD_CONV_DIM // LANES
N_IN_TILES = N_Z_TILES + N_CONV_TILES + 1
B_TILE0 = N_X_TILES
C_TILE0 = N_X_TILES + SSD_N_GROUPS
HALO_ROWS = 8
SSD_CHUNKS_PER_STEP = 4
N_STAGE = 4
P_X0 = N_Z_TILES
N_P_TILES = N_Z_TILES + N_X_TILES

ATTN_N_Q_HEADS = 16
ATTN_N_KV_HEADS = 4
ATTN_REP = 4
ATTN_WINDOW = 128
ATTN_BLOCKS_PER_STEP = 16
ATTN_Q_DIM = ATTN_N_Q_HEADS * HEAD_DIM
ATTN_KV_DIM = ATTN_N_KV_HEADS * HEAD_DIM
ATTN_QKV_DIM = ATTN_Q_DIM + 2 * ATTN_KV_DIM

D_FF = 4096
FF_CHUNK = 1024

ROW_TILE = 512
VMEM_LIMIT = 56 * 1024 * 1024


def _rms(x, gain):
    ms = jnp.mean(x * x, axis=-1, keepdims=True)
    return x * lax.rsqrt(ms + NORM_EPS) * gain


def _sigmoid(x):
    return 1.0 / (1.0 + jnp.exp2(x * (-LOG2_E)))


def _resident(shape):
    nd = len(shape)
    return pl.BlockSpec(shape, lambda *_: (0,) * nd, pipeline_mode=pl.Buffered(1))


def _resident_layer(stacked_shape, layer):
    return pl.BlockSpec((None,) + tuple(stacked_shape[1:]), lambda *_: (layer, 0, 0),
                        pipeline_mode=pl.Buffered(1))


def _params(*sem):
    return pltpu.CompilerParams(dimension_semantics=sem, vmem_limit_bytes=VMEM_LIMIT)


def _ssd_in_proj_kernel(x_ref, g_ref, w_ref, wdt_ref, cw_ref, cb_ref, dtb_ref, alog_ref,
                        p_ref, c_ref, bt_ref, cum_ref, cumt_ref, dtt_ref, wendt_ref, halo_ref, stage_ref):
    R = ROW_TILE
    T = SSD_CHUNK

    @pl.when(pl.program_id(0) % (SEQ // ROW_TILE) == 0)
    def _():
        halo_ref[...] = jnp.zeros_like(halo_ref)

    xn = _rms(x_ref[...], g_ref[...]).astype(BF16)
    pair = 2 * LANES

    dt_in = jnp.dot(xn, wdt_ref[...], preferred_element_type=F32) + dtb_ref[...]
    dt = jnp.maximum(dt_in, 0.0) + jnp.log1p(jnp.exp(-jnp.abs(dt_in)))
    aq = dt * (-jnp.exp(alog_ref[...]))
    tri = (lax.broadcasted_iota(jnp.int32, (T, T), 0)
           >= lax.broadcasted_iota(jnp.int32, (T, T), 1)).astype(BF16)
    aq_hi = aq.astype(BF16)
    rem = aq - aq_hi.astype(F32)
    aq_mid = rem.astype(BF16)
    aq_lo = (rem - aq_mid.astype(F32)).astype(BF16)
    for c in range(R // T):
        rows = slice(c * T, (c + 1) * T)
        cum = (jnp.dot(tri, aq_hi[rows], preferred_element_type=F32)
               + jnp.dot(tri, aq_mid[rows], preferred_element_type=F32)
               + jnp.dot(tri, aq_lo[rows], preferred_element_type=F32))
        cum_last = cum[T - 1:T, :]
        cum2 = cum * LOG2_E
        cum_ref[rows, :] = cum2
        cumt_ref[:, rows] = cum2.T
        dtt_ref[:, rows] = dt[rows].T
        wendt_ref[:, rows] = (dt[rows] * jnp.exp(cum_last - cum)).T

    def z_pair(t):
        acc = jnp.dot(xn, w_ref[:, t * LANES:t * LANES + pair], preferred_element_type=F32)
        gate = acc * _sigmoid(acc)
        p_ref[t] = gate[:, :LANES]
        p_ref[t + 1] = gate[:, LANES:]

    def conv_silu(raw, kc):
        w = cw_ref[kc]
        slot = kc % N_STAGE
        stage_ref[slot, 0:HALO_ROWS, :] = halo_ref[kc]
        stage_ref[slot, HALO_ROWS:, :] = raw
        acc = cb_ref[kc] + w[SSD_CONV_WIDTH - 1:SSD_CONV_WIDTH] * raw
        for d in range(1, SSD_CONV_WIDTH):
            shifted = stage_ref[slot, HALO_ROWS - d:HALO_ROWS - d + R, :]
            acc = acc + w[SSD_CONV_WIDTH - 1 - d:SSD_CONV_WIDTH - d] * shifted
        halo_ref[kc] = raw[R - HALO_ROWS:]
        return acc * _sigmoid(acc)

    def put_conv(kc, val):
        if kc < B_TILE0:
            p_ref[P_X0 + kc] = val
        elif kc < C_TILE0:
            for c in range(R // T):
                bt_ref[kc - B_TILE0, :, c * T:(c + 1) * T] = val[c * T:(c + 1) * T].T.astype(bt_ref.dtype)
        else:
            c_ref[kc - C_TILE0] = val.astype(c_ref.dtype)

    def conv_pair(t):
        col = (N_Z_TILES + t) * LANES
        acc = jnp.dot(xn, w_ref[:, col:col + pair], preferred_element_type=F32)
        put_conv(t, conv_silu(acc[:, :LANES], t))
        put_conv(t + 1, conv_silu(acc[:, LANES:], t + 1))

    for i in range(N_Z_TILES // 2):
        conv_pair(4 * i)
        z_pair(2 * i)
        conv_pair(4 * i + 2)


def _ssd_in_proj(x2d, gain, w_in, w_dt, conv_w, conv_b, dt_bias, a_log):
    m = x2d.shape[0]
    t_major = pl.BlockSpec((ROW_TILE, LANES), lambda i: (i, 0))
    h_major = pl.BlockSpec((LANES, ROW_TILE), lambda i: (0, i))
    return pl.pallas_call(
        _ssd_in_proj_kernel,
        grid=(m // ROW_TILE,),
        in_specs=[
            pl.BlockSpec((ROW_TILE, D_MODEL), lambda i: (i, 0)),
            _resident((1, D_MODEL)),
            _resident(w_in.shape),
            _resident(w_dt.shape),
            _resident((N_CONV_TILES, SSD_CONV_WIDTH, LANES)),
            _resident((N_CONV_TILES, 1, LANES)),
            _resident((1, LANES)),
            _resident((1, LANES)),
        ],
        out_specs=[
            pl.BlockSpec((N_P_TILES, ROW_TILE, LANES), lambda i: (0, i, 0)),
            pl.BlockSpec((SSD_N_GROUPS, ROW_TILE, LANES), lambda i: (0, i, 0)),
            pl.BlockSpec((SSD_N_GROUPS, SSD_D_STATE, ROW_TILE), lambda i: (0, 0, i)),
            t_major, h_major, h_major, h_major,
        ],
        out_shape=[
            jax.ShapeDtypeStruct((N_P_TILES, m, LANES), F32),
            jax.ShapeDtypeStruct((SSD_N_GROUPS, m, LANES), F32),
            jax.ShapeDtypeStruct((SSD_N_GROUPS, SSD_D_STATE, m), F32),
            jax.ShapeDtypeStruct((m, LANES), F32),
            jax.ShapeDtypeStruct((LANES, m), F32),
            jax.ShapeDtypeStruct((LANES, m), F32),
            jax.ShapeDtypeStruct((LANES, m), F32),
        ],
        scratch_shapes=[
            pltpu.VMEM((N_CONV_TILES, HALO_ROWS, LANES), F32),
            pltpu.VMEM((N_STAGE, HALO_ROWS + ROW_TILE, LANES), F32),
        ],
        compiler_params=_params("arbitrary"),
        name="ssd_in_proj",
    )(x2d, gain, w_in, w_dt, conv_w, conv_b, dt_bias, a_log)


def _ssd_kernel(p_ref, c_ref, bt_ref, cum_ref, cumt_ref, dtt_ref, wendt_ref, dskip_ref, nw_ref, o_ref, state_ref):
    T = SSD_CHUNK
    lane = lax.broadcasted_iota(jnp.int32, (T, LANES), 1)
    row = lax.broadcasted_iota(jnp.int32, (T, LANES), 0)
    lane_lo = lane < HEAD_DIM
    causal = row >= lane

    @pl.when(pl.program_id(1) == 0)
    def _():
        state_ref[...] = jnp.zeros_like(state_ref)

    def group(rows, dec, g):
        cm = c_ref[g, rows, :].astype(BF16)
        bt = bt_ref[g, :, rows].astype(BF16)
        cb = jnp.dot(cm, bt, preferred_element_type=F32).astype(BF16)
        gated = []
        for half in range(2):
            kx = 2 * g + half
            x = p_ref[P_X0 + kx, rows, :]
            st = state_ref[kx]
            xb = x.astype(BF16)
            lhs_y = []
            lhs_s = []
            for i in range(2):
                h = 2 * kx + i
                crow = cumt_ref[h:h + 1, rows]
                cbc = jnp.broadcast_to(cum_ref[rows, h:h + 1], (T, LANES))
                decay = jnp.where(causal, jnp.exp2(cbc - crow), 0.0).astype(BF16)
                lhs_y.append(jnp.concatenate(
                    [cb * decay * dtt_ref[h:h + 1, rows].astype(BF16), cm * jnp.exp2(cbc).astype(BF16)],
                    axis=1))
                lhs_s.append(bt * wendt_ref[h:h + 1, rows].astype(BF16))
            yy = jnp.dot(jnp.concatenate(lhs_y, axis=0), jnp.concatenate([xb, st.astype(BF16)], axis=0),
                         preferred_element_type=F32)
            y = jnp.where(lane_lo, yy[:T], yy[T:]) + dskip_ref[kx] * x
            ds = jnp.dot(jnp.concatenate(lhs_s, axis=0), xb, preferred_element_type=F32)
            d0 = jnp.broadcast_to(dec[:, 2 * kx:2 * kx + 1], (1, LANES))
            d1 = jnp.broadcast_to(dec[:, 2 * kx + 1:2 * kx + 2], (1, LANES))
            state_ref[kx] = st * jnp.where(lane_lo[0:1], d0, d1) + jnp.where(
                lane_lo, ds[:SSD_D_STATE], ds[SSD_D_STATE:])
            gated.append(y * p_ref[kx, rows, :])
        ss = (jnp.sum(gated[0] * gated[0], axis=-1, keepdims=True)
              + jnp.sum(gated[1] * gated[1], axis=-1, keepdims=True))
        r = lax.rsqrt(ss * (1.0 / (2 * LANES)) + NORM_EPS)
        for half in range(2):
            kx = 2 * g + half
            o_ref[kx, rows, :] = (gated[half] * r * nw_ref[kx]).astype(o_ref.dtype)

    for ci in range(SSD_CHUNKS_PER_STEP):
        rows = slice(ci * T, (ci + 1) * T)
        dec = jnp.exp2(cum_ref[(ci + 1) * T - 1:(ci + 1) * T, :])
        for g in range(SSD_N_GROUPS):
            group(rows, dec, g)


def _ssd_core(p, c, bt, cum, cumt, dtt, wendt, dskip, norm_w, batch):
    m = p.shape[1]
    R = SSD_CHUNKS_PER_STEP * SSD_CHUNK
    n_steps = m // batch // R
    h_major = pl.BlockSpec((LANES, R), lambda b, c: (0, b * n_steps + c))
    return pl.pallas_call(
        _ssd_kernel,
        grid=(batch, n_steps),
        in_specs=[
            pl.BlockSpec((N_P_TILES, R, LANES), lambda b, c: (0, b * n_steps + c, 0)),
            pl.BlockSpec((SSD_N_GROUPS, R, LANES), lambda b, c: (0, b * n_steps + c, 0)),
            pl.BlockSpec((SSD_N_GROUPS, SSD_D_STATE, R), lambda b, c: (0, 0, b * n_steps + c)),
            pl.BlockSpec((R, LANES), lambda b, c: (b * n_steps + c, 0)),
            h_major, h_major, h_major,
            _resident((N_X_TILES, 1, LANES)),
            _resident((N_X_TILES, 1, LANES)),
        ],
        out_specs=pl.BlockSpec((N_X_TILES, R, LANES), lambda b, c: (0, b * n_steps + c, 0)),
        out_shape=jax.ShapeDtypeStruct((N_X_TILES, m, LANES), BF16),
        scratch_shapes=[pltpu.VMEM((N_X_TILES, SSD_D_STATE, LANES), F32)],
        compiler_params=_params("arbitrary", "arbitrary"),
        name="ssd_core",
    )(p, c, bt, cum, cumt, dtt, wendt, dskip, norm_w)


def _mlp_residual(h, g1_ref, wu_ref, wd_ref, g2_ref):
    u = _rms(h, g1_ref[...]).astype(BF16)
    acc = None
    for c in range(0, D_FF, FF_CHUNK):
        a = jnp.dot(u, wu_ref[:, c:c + FF_CHUNK], preferred_element_type=F32)
        a = jnp.square(jnp.maximum(a, 0.0)).astype(BF16)
        part = jnp.dot(a, wd_ref[c:c + FF_CHUNK, :], preferred_element_type=F32)
        acc = part if acc is None else acc + part
    return h + _rms(acc, g2_ref[...])


def _ssd_tail_kernel(y_ref, wo_ref, h_ref, gp_ref, g1_ref, wu_ref, wd_ref, g2_ref,
                     gq_ref, wq_ref, bq_ref, o_ref, qkv_ref):
    y = jnp.concatenate([y_ref[k] for k in range(N_X_TILES)], axis=1)
    mix = jnp.dot(y, wo_ref[...], preferred_element_type=F32)
    h = _mlp_residual(h_ref[...] + _rms(mix, gp_ref[...]), g1_ref, wu_ref, wd_ref, g2_ref)
    o_ref[...] = h
    xn = _rms(h, gq_ref[...]).astype(BF16)
    qkv = jnp.dot(xn, wq_ref[...], preferred_element_type=F32) + bq_ref[...]
    qkv_ref[...] = qkv.astype(qkv_ref.dtype)


def _attn_tail_kernel(a_ref, wo_ref, bo_ref, h_ref, gp_ref, g1_ref, wu_ref, wd_ref, g2_ref, o_ref):
    mix = jnp.dot(a_ref[...], wo_ref[...], preferred_element_type=F32) + bo_ref[...]
    o_ref[...] = _mlp_residual(h_ref[...] + _rms(mix, gp_ref[...]), g1_ref, wu_ref, wd_ref, g2_ref)


_ROWS = pl.BlockSpec((ROW_TILE, D_MODEL), lambda i: (i, 0))
_GAIN = (1, D_MODEL)


def _ssd_tail(y_tiles, w_out, h, g_post, g1, wu, wd, layer, g2, g_qkv, w_qkv, b_qkv):
    m = h.shape[0]
    return pl.pallas_call(
        _ssd_tail_kernel,
        grid=(m // ROW_TILE,),
        in_specs=[
            pl.BlockSpec((N_X_TILES, ROW_TILE, LANES), lambda i: (0, i, 0)),
            _resident(w_out.shape), _ROWS, _resident(_GAIN),
            _resident(_GAIN), _resident_layer(wu.shape, layer), _resident_layer(wd.shape, layer), _resident(_GAIN),
            _resident(_GAIN), _resident(w_qkv.shape), _resident(b_qkv.shape),
        ],
        out_specs=[_ROWS, pl.BlockSpec((ROW_TILE, ATTN_QKV_DIM), lambda i: (i, 0))],
        out_shape=[jax.ShapeDtypeStruct((m, D_MODEL), F32),
                   jax.ShapeDtypeStruct((m, ATTN_QKV_DIM), BF16)],
        compiler_params=_params("arbitrary"),
        name="ssd_tail_mlp_qkv",
    )(y_tiles, w_out, h, g_post, g1, wu, wd, g2, g_qkv, w_qkv, b_qkv)


def _attn_tail(a, w_o, b_o, h, g_post, g1, wu, wd, layer, g2):
    m = h.shape[0]
    return pl.pallas_call(
        _attn_tail_kernel,
        grid=(m // ROW_TILE,),
        in_specs=[
            pl.BlockSpec((ROW_TILE, ATTN_Q_DIM), lambda i: (i, 0)),
            _resident(w_o.shape), _resident(_GAIN), _ROWS, _resident(_GAIN),
            _resident(_GAIN), _resident_layer(wu.shape, layer), _resident_layer(wd.shape, layer), _resident(_GAIN),
        ],
        out_specs=_ROWS,
        out_shape=jax.ShapeDtypeStruct((m, D_MODEL), F32),
        compiler_params=_params("arbitrary"),
        name="attn_tail_mlp",
    )(a, w_o, b_o, h, g_post, g1, wu, wd, g2)


def _attn_kernel(sink_ref, q_ref, kp_ref, kc_ref, vp_ref, vc_ref, o_ref):
    W = ATTN_WINDOW
    PACK = 16
    top = lax.broadcasted_iota(jnp.int32, (PACK, 2 * LANES), 0) == 0
    zero = jnp.zeros((), BF16)

    def window(prev_ref, cur_ref, blk):
        prev = prev_ref[...] if blk == 0 else cur_ref[(blk - 1) * W:blk * W, :]
        return jnp.concatenate(
            [jnp.where(top, zero, prev[:PACK]), prev[PACK:], cur_ref[blk * W:(blk + 1) * W, :]], axis=0)

    qrow = lax.broadcasted_iota(jnp.int32, (2 * W, 2 * W), 0)
    kpos = lax.broadcasted_iota(jnp.int32, (2 * W, 2 * W), 1)
    qpos = jnp.where(qrow >= W, qrow - W, qrow) + W
    rel = qpos - kpos
    band = (rel >= 0) & (rel < W)
    lane = lax.broadcasted_iota(jnp.int32, (W, LANES), 1)
    lane_lo = lane < HEAD_DIM
    q_extra = jnp.concatenate([jnp.where(lane < 3, 1.0, 0.0),
                               jnp.where((lane >= 3) & (lane < 6), 1.0, 0.0)], axis=0).astype(BF16)
    lane1 = lax.broadcasted_iota(jnp.int32, (1, LANES), 1)
    is_piece0 = (lane1 == 0) | (lane1 == 3)
    is_piece1 = (lane1 == 1) | (lane1 == 4)
    k_extras = []
    for t in range(ATTN_Q_DIM // LANES):
        sv = sink_ref[t]
        s_hi = sv.astype(BF16).astype(F32)
        s_mid = (sv - s_hi).astype(BF16).astype(F32)
        s_lo = (sv - s_hi - s_mid).astype(BF16).astype(F32)
        srow = jnp.where(is_piece0, s_hi, jnp.where(is_piece1, s_mid, s_lo))
        k_extras.append(jnp.concatenate(
            [jnp.where(top[:, :LANES], jnp.broadcast_to(srow, (PACK, LANES)), 0.0).astype(BF16),
             jnp.zeros((2 * W - PACK, LANES), BF16)], axis=0))

    ones_keys = jnp.ones((2 * W, LANES), BF16)
    for blk in range(ATTN_BLOCKS_PER_STEP):
        rows = slice(blk * W, (blk + 1) * W)
        kcat = window(kp_ref, kc_ref, blk)
        vcat = window(vp_ref, vc_ref, blk)
        if blk == 0:
            kmin = jnp.where(pl.program_id(1) == 0, W, 0)
            valid = (band & (kpos >= kmin)) | (kpos == 0)
        else:
            valid = band | (kpos == 0)
        bias = jnp.where(valid, 0.0, -jnp.inf)
        for t in range(ATTN_Q_DIM // LANES):
            pair = t // ATTN_REP
            qt = q_ref[rows, t * LANES:(t + 1) * LANES] * jnp.asarray(HEAD_DIM ** -0.5, BF16)
            qs = jnp.concatenate([jnp.where(lane_lo, qt, zero), jnp.where(lane_lo, zero, qt)], axis=0)
            kt = jnp.concatenate([kcat[:, pair * LANES:(pair + 1) * LANES], k_extras[t]], axis=1)
            vt = jnp.concatenate([vcat[:, pair * LANES:(pair + 1) * LANES], ones_keys], axis=1)
            s = lax.dot_general(jnp.concatenate([qs, q_extra], axis=1), kt, (((1,), (1,)), ((), ())),
                                preferred_element_type=F32) + bias
            mx = jnp.max(s, axis=-1, keepdims=True)
            e = jnp.exp((s - mx).astype(BF16))
            pv = jnp.dot(e, vt, preferred_element_type=F32)
            pv = pv[:, :LANES] * (1.0 / pv[:, LANES:])
            o_ref[rows, t * LANES:(t + 1) * LANES] = jnp.where(lane_lo, pv[:W], pv[W:]).astype(o_ref.dtype)


def _attention(qkv, sinks, batch):
    m = qkv.shape[0]
    W = ATTN_WINDOW
    R = ATTN_BLOCKS_PER_STEP * W
    nb = m // batch // W
    n_steps = nb // ATTN_BLOCKS_PER_STEP
    k_col = ATTN_Q_DIM // ATTN_KV_DIM
    v_col = k_col + 1

    def cur(b, i):
        return b * n_steps + i

    def prev(b, i):
        return jnp.maximum(b * nb + ATTN_BLOCKS_PER_STEP * i - 1, 0)

    return pl.pallas_call(
        _attn_kernel,
        grid=(batch, n_steps),
        in_specs=[
            _resident(sinks.shape),
            pl.BlockSpec((R, ATTN_Q_DIM), lambda b, i: (cur(b, i), 0)),
            pl.BlockSpec((W, ATTN_KV_DIM), lambda b, i: (prev(b, i), k_col)),
            pl.BlockSpec((R, ATTN_KV_DIM), lambda b, i: (cur(b, i), k_col)),
            pl.BlockSpec((W, ATTN_KV_DIM), lambda b, i: (prev(b, i), v_col)),
            pl.BlockSpec((R, ATTN_KV_DIM), lambda b, i: (cur(b, i), v_col)),
        ],
        out_specs=pl.BlockSpec((R, ATTN_Q_DIM), lambda b, i: (cur(b, i), 0)),
        out_shape=jax.ShapeDtypeStruct((m, ATTN_Q_DIM), BF16),
        compiler_params=_params("arbitrary", "arbitrary"),
        name="swa_sink_attention",
    )(sinks, qkv, qkv, qkv, qkv, qkv)


def _permute_q_heads(v, axis):
    inner = v.shape[axis] // ATTN_N_Q_HEADS
    shape = v.shape[:axis] + (ATTN_N_KV_HEADS // 2, 2, ATTN_REP, inner) + v.shape[axis + 1:]
    order = list(range(len(shape)))
    order[axis + 1], order[axis + 2] = axis + 2, axis + 1
    return v.reshape(shape).transpose(order).reshape(v.shape)


def _lane_tiles(v, n_tiles):
    return v.astype(F32).reshape(n_tiles, 1, LANES)


def kernel(x, ssd_w_in, ssd_conv_w, ssd_conv_b, ssd_dt_bias, ssd_a_log, ssd_d, ssd_norm_w, ssd_w_out, attn_w_qkv, attn_b_qkv, attn_sinks, attn_w_o, attn_b_o, mlp_w_up, mlp_w_down, mix_pre_norm, mix_post_norm, ffn_pre_norm, ffn_post_norm):
    batch, seq, d = x.shape
    m = batch * seq
    h = x.reshape(m, d)
    row = lambda v: v.astype(F32).reshape(1, -1)

    w_in = ssd_w_in[0].astype(BF16)
    w_dt = jnp.pad(ssd_w_in[0][:, SSD_IN_DIM - SSD_N_HEADS:], ((0, 0), (0, LANES - SSD_N_HEADS))).astype(BF16)
    conv_w = ssd_conv_w[0].astype(F32).reshape(SSD_CONV_WIDTH, N_CONV_TILES, LANES).transpose(1, 0, 2)
    pad_heads = lambda v: jnp.pad(v.astype(F32), (0, LANES - SSD_N_HEADS)).reshape(1, LANES)
    assert seq == SEQ and seq % ROW_TILE == 0
    p, c, bt, cum, cumt, dtt, wendt = _ssd_in_proj(
        h, row(mix_pre_norm[0]), w_in, w_dt, conv_w, _lane_tiles(ssd_conv_b[0], N_CONV_TILES),
        pad_heads(ssd_dt_bias[0]), pad_heads(ssd_a_log[0]))
    y = _ssd_core(
        p, c, bt, cum, cumt, dtt, wendt, _lane_tiles(jnp.repeat(ssd_d[0], HEAD_DIM), N_X_TILES),
        _lane_tiles(ssd_norm_w[0], N_X_TILES), batch)
    wq, wkv = attn_w_qkv[0][:, :ATTN_Q_DIM], attn_w_qkv[0][:, ATTN_Q_DIM:]
    bq, bkv = attn_b_qkv[0][:ATTN_Q_DIM], attn_b_qkv[0][ATTN_Q_DIM:]
    w_qkv = jnp.concatenate([_permute_q_heads(wq, 1), wkv], axis=1).astype(BF16)
    b_qkv = jnp.concatenate([_permute_q_heads(bq, 0), bkv]).astype(F32).reshape(1, -1)
    w_up, w_down = mlp_w_up.astype(BF16), mlp_w_down.astype(BF16)
    h, qkv = _ssd_tail(
        y, ssd_w_out[0].astype(BF16), h, row(mix_post_norm[0]),
        row(ffn_pre_norm[0]), w_up, w_down, 0, row(ffn_post_norm[0]),
        row(mix_pre_norm[1]), w_qkv, b_qkv)

    sink_pairs = _permute_q_heads(attn_sinks[0], 0).astype(F32).reshape(ATTN_Q_DIM // LANES, 2, 1)
    sink_rows = jnp.pad(jnp.broadcast_to(sink_pairs, sink_pairs.shape[:2] + (3,)).reshape(-1, 1, 6),
                        ((0, 0), (0, 0), (0, LANES - 6)))
    a = _attention(qkv, sink_rows, batch)
    h = _attn_tail(
        a, _permute_q_heads(attn_w_o[0], 0).astype(BF16), row(attn_b_o[0]), h, row(mix_post_norm[1]),
        row(ffn_pre_norm[1]), w_up, w_down, 1, row(ffn_post_norm[1]))
    return h.reshape(batch, seq, d)
```

```python
import functools

import jax
import jax.numpy as jnp
from jax import lax
from jax.experimental import pallas as pl
from jax.experimental.pallas import tpu as pltpu

F32 = jnp.float32
BF16 = jnp.bfloat16

D_MODEL = 1024
SEQ = 4096
NORM_EPS = 1e-6
LOG2_E = 1.4426950408889634

LANES = 128
HEAD_DIM = 64

SSD_D_INNER = 2048
SSD_N_HEADS = 32
SSD_N_GROUPS = 8
SSD_D_STATE = 128
SSD_CONV_WIDTH = 4
SSD_CHUNK = 128
SSD_CONV_DIM = SSD_D_INNER + 2 * SSD_N_GROUPS * SSD_D_STATE
SSD_IN_DIM = SSD_D_INNER + SSD_CONV_DIM + SSD_N_HEADS
N_Z_TILES = SSD_D_INNER // LANES
N_X_TILES = SSD_D_INNER // LANES
N_CONV_TILES = SSD_CONV_DIM // LANES
N_IN_TILES = N_Z_TILES + N_CONV_TILES + 1
B_TILE0 = N_X_TILES
C_TILE0 = N_X_TILES + SSD_N_GROUPS
HALO_ROWS = 8
SSD_CHUNKS_PER_STEP = 4
N_STAGE = 4
P_X0 = N_Z_TILES
N_P_TILES = N_Z_TILES + N_X_TILES

ATTN_N_Q_HEADS = 16
ATTN_N_KV_HEADS = 4
ATTN_REP = 4
ATTN_WINDOW = 128
ATTN_BLOCKS_PER_STEP = 16
ATTN_Q_DIM = ATTN_N_Q_HEADS * HEAD_DIM
ATTN_KV_DIM = ATTN_N_KV_HEADS * HEAD_DIM
ATTN_QKV_DIM = ATTN_Q_DIM + 2 * ATTN_KV_DIM

D_FF = 4096
FF_CHUNK = 1024

ROW_TILE = 512
VMEM_LIMIT = 56 * 1024 * 1024


def _rms(x, gain):
    ms = jnp.mean(x * x, axis=-1, keepdims=True)
    return x * lax.rsqrt(ms + NORM_EPS) * gain


def _sigmoid(x):
    return 1.0 / (1.0 + jnp.exp2(x * (-LOG2_E)))


def _resident(shape):
    nd = len(shape)
    return pl.BlockSpec(shape, lambda *_: (0,) * nd, pipeline_mode=pl.Buffered(1))


def _resident_layer(stacked_shape, layer):
    return pl.BlockSpec((None,) + tuple(stacked_shape[1:]), lambda *_: (layer, 0, 0),
                        pipeline_mode=pl.Buffered(1))


def _params(*sem):
    return pltpu.CompilerParams(dimension_semantics=sem, vmem_limit_bytes=VMEM_LIMIT)


def _ssd_in_proj_kernel(x_ref, g_ref, w_ref, wdt_ref, cw_ref, cb_ref, dtb_ref, alog_ref,
                        p_ref, c_ref, bt_ref, cum_ref, cumt_ref, dtt_ref, wendt_ref, halo_ref, stage_ref):
    R = ROW_TILE
    T = SSD_CHUNK

    @pl.when(pl.program_id(0) % (SEQ // ROW_TILE) == 0)
    def _():
        halo_ref[...] = jnp.zeros_like(halo_ref)

    xn = _rms(x_ref[...], g_ref[...]).astype(BF16)
    pair = 2 * LANES

    dt_in = jnp.dot(xn, wdt_ref[...], preferred_element_type=F32) + dtb_ref[...]
    dt = jnp.maximum(dt_in, 0.0) + jnp.log1p(jnp.exp(-jnp.abs(dt_in)))
    aq = dt * (-jnp.exp(alog_ref[...]))
    tri = (lax.broadcasted_iota(jnp.int32, (T, T), 0)
           >= lax.broadcasted_iota(jnp.int32, (T, T), 1)).astype(BF16)
    aq_hi = aq.astype(BF16)
    rem = aq - aq_hi.astype(F32)
    aq_mid = rem.astype(BF16)
    aq_lo = (rem - aq_mid.astype(F32)).astype(BF16)
    for c in range(R // T):
        rows = slice(c * T, (c + 1) * T)
        cum = (jnp.dot(tri, aq_hi[rows], preferred_element_type=F32)
               + jnp.dot(tri, aq_mid[rows], preferred_element_type=F32)
               + jnp.dot(tri, aq_lo[rows], preferred_element_type=F32))
        cum_last = cum[T - 1:T, :]
        cum2 = cum * LOG2_E
        cum_ref[rows, :] = cum2
        cumt_ref[:, rows] = cum2.T
        dtt_ref[:, rows] = dt[rows].T
        wendt_ref[:, rows] = (dt[rows] * jnp.exp(cum_last - cum)).T

    def z_pair(t):
        acc = jnp.dot(xn, w_ref[:, t * LANES:t * LANES + pair], preferred_element_type=F32)
        gate = acc * _sigmoid(acc)
        p_ref[t] = gate[:, :LANES]
        p_ref[t + 1] = gate[:, LANES:]

    def conv_silu(raw, kc):
        w = cw_ref[kc]
        slot = kc % N_STAGE
        stage_ref[slot, 0:HALO_ROWS, :] = halo_ref[kc]
        stage_ref[slot, HALO_ROWS:, :] = raw
        acc = cb_ref[kc] + w[SSD_CONV_WIDTH - 1:SSD_CONV_WIDTH] * raw
        for d in range(1, SSD_CONV_WIDTH):
            shifted = stage_ref[slot, HALO_ROWS - d:HALO_ROWS - d + R, :]
            acc = acc + w[SSD_CONV_WIDTH - 1 - d:SSD_CONV_WIDTH - d] * shifted
        halo_ref[kc] = raw[R - HALO_ROWS:]
        return acc * _sigmoid(acc)

    def put_conv(kc, val):
        if kc < B_TILE0:
            p_ref[P_X0 + kc] = val
        elif kc < C_TILE0:
            for c in range(R // T):
                bt_ref[kc - B_TILE0, :, c * T:(c + 1) * T] = val[c * T:(c + 1) * T].T.astype(bt_ref.dtype)
        else:
            c_ref[kc - C_TILE0] = val.astype(c_ref.dtype)

    def conv_pair(t):
        col = (N_Z_TILES + t) * LANES
        acc = jnp.dot(xn, w_ref[:, col:col + pair], preferred_element_type=F32)
        put_conv(t, conv_silu(acc[:, :LANES], t))
        put_conv(t + 1, conv_silu(acc[:, LANES:], t + 1))

    for i in range(N_Z_TILES // 2):
        conv_pair(4 * i)
        z_pair(2 * i)
        conv_pair(4 * i + 2)


def _ssd_in_proj(x2d, gain, w_in, w_dt, conv_w, conv_b, dt_bias, a_log):
    m = x2d.shape[0]
    t_major = pl.BlockSpec((ROW_TILE, LANES), lambda i: (i, 0))
    h_major = pl.BlockSpec((LANES, ROW_TILE), lambda i: (0, i))
    return pl.pallas_call(
        _ssd_in_proj_kernel,
        grid=(m // ROW_TILE,),
        in_specs=[
            pl.BlockSpec((ROW_TILE, D_MODEL), lambda i: (i, 0)),
            _resident((1, D_MODEL)),
            _resident(w_in.shape),
            _resident(w_dt.shape),
            _resident((N_CONV_TILES, SSD_CONV_WIDTH, LANES)),
            _resident((N_CONV_TILES, 1, LANES)),
            _resident((1, LANES)),
            _resident((1, LANES)),
        ],
        out_specs=[
            pl.BlockSpec((N_P_TILES, ROW_TILE, LANES), lambda i: (0, i, 0)),
            pl.BlockSpec((SSD_N_GROUPS, ROW_TILE, LANES), lambda i: (0, i, 0)),
            pl.BlockSpec((SSD_N_GROUPS, SSD_D_STATE, ROW_TILE), lambda i: (0, 0, i)),
            t_major, h_major, h_major, h_major,
        ],
        out_shape=[
            jax.ShapeDtypeStruct((N_P_TILES, m, LANES), F32),
            jax.ShapeDtypeStruct((SSD_N_GROUPS, m, LANES), F32),
            jax.ShapeDtypeStruct((SSD_N_GROUPS, SSD_D_STATE, m), F32),
            jax.ShapeDtypeStruct((m, LANES), F32),
            jax.ShapeDtypeStruct((LANES, m), F32),
            jax.ShapeDtypeStruct((LANES, m), F32),
            jax.ShapeDtypeStruct((LANES, m), F32),
        ],
        scratch_shapes=[
            pltpu.VMEM((N_CONV_TILES, HALO_ROWS, LANES), F32),
            pltpu.VMEM((N_STAGE, HALO_ROWS + ROW_TILE, LANES), F32),
        ],
        compiler_params=_params("arbitrary"),
        name="ssd_in_proj",
    )(x2d, gain, w_in, w_dt, conv_w, conv_b, dt_bias, a_log)


def _ssd_kernel(p_ref, c_ref, bt_ref, cum_ref, cumt_ref, dtt_ref, wendt_ref, dskip_ref, nw_ref, o_ref, state_ref):
    T = SSD_CHUNK
    lane = lax.broadcasted_iota(jnp.int32, (T, LANES), 1)
    row = lax.broadcasted_iota(jnp.int32, (T, LANES), 0)
    lane_lo = lane < HEAD_DIM
    causal = row >= lane

    @pl.when(pl.program_id(1) == 0)
    def _():
        state_ref[...] = jnp.zeros_like(state_ref)

    def group(rows, dec, g):
        cm = c_ref[g, rows, :].astype(BF16)
        bt = bt_ref[g, :, rows].astype(BF16)
        cb = jnp.dot(cm, bt, preferred_element_type=F32).astype(BF16)
        gated = []
        for half in range(2):
            kx = 2 * g + half
            x = p_ref[P_X0 + kx, rows, :]
            st = state_ref[kx]
            xb = x.astype(BF16)
            lhs_y = []
            lhs_s = []
            for i in range(2):
                h = 2 * kx + i
                crow = cumt_ref[h:h + 1, rows]
                cbc = jnp.broadcast_to(cum_ref[rows, h:h + 1], (T, LANES))
                decay = jnp.where(causal, jnp.exp2(cbc - crow), 0.0).astype(BF16)
                lhs_y.append(jnp.concatenate(
                    [cb * decay * jnp.broadcast_to(dtt_ref[h:h + 1, rows], (T, LANES)).astype(BF16),
                     cm * jnp.exp2(cbc).astype(BF16)],
                    axis=1))
                lhs_s.append(bt * jnp.broadcast_to(wendt_ref[h:h + 1, rows], (SSD_D_STATE, T)).astype(BF16))
            yy = jnp.dot(jnp.concatenate(lhs_y, axis=0), jnp.concatenate([xb, st.astype(BF16)], axis=0),
                         preferred_element_type=F32)
            y = jnp.where(lane_lo, yy[:T], yy[T:]) + dskip_ref[kx] * x
            ds = jnp.dot(jnp.concatenate(lhs_s, axis=0), xb, preferred_element_type=F32)
            d0 = jnp.broadcast_to(dec[:, 2 * kx:2 * kx + 1], (1, LANES))
            d1 = jnp.broadcast_to(dec[:, 2 * kx + 1:2 * kx + 2], (1, LANES))
            state_ref[kx] = st * jnp.where(lane_lo[0:1], d0, d1) + jnp.where(
                lane_lo, ds[:SSD_D_STATE], ds[SSD_D_STATE:])
            gated.append(y * p_ref[kx, rows, :])
        ss = (jnp.sum(gated[0] * gated[0], axis=-1, keepdims=True)
              + jnp.sum(gated[1] * gated[1], axis=-1, keepdims=True))
        r = lax.rsqrt(ss * (1.0 / (2 * LANES)) + NORM_EPS)
        for half in range(2):
            kx = 2 * g + half
            o_ref[kx, rows, :] = (gated[half] * r * nw_ref[kx]).astype(o_ref.dtype)

    for ci in range(SSD_CHUNKS_PER_STEP):
        rows = slice(ci * T, (ci + 1) * T)
        dec = jnp.exp2(cum_ref[(ci + 1) * T - 1:(ci + 1) * T, :])
        for g in range(SSD_N_GROUPS):
            group(rows, dec, g)


def _ssd_core(p, c, bt, cum, cumt, dtt, wendt, dskip, norm_w, batch):
    m = p.shape[1]
    R = SSD_CHUNKS_PER_STEP * SSD_CHUNK
    n_steps = m // batch // R
    h_major = pl.BlockSpec((LANES, R), lambda b, c: (0, b * n_steps + c))
    return pl.pallas_call(
        _ssd_kernel,
        grid=(batch, n_steps),
        in_specs=[
            pl.BlockSpec((N_P_TILES, R, LANES), lambda b, c: (0, b * n_steps + c, 0)),
            pl.BlockSpec((SSD_N_GROUPS, R, LANES), lambda b, c: (0, b * n_steps + c, 0)),
            pl.BlockSpec((SSD_N_GROUPS, SSD_D_STATE, R), lambda b, c: (0, 0, b * n_steps + c)),
            pl.BlockSpec((R, LANES), lambda b, c: (b * n_steps + c, 0)),
            h_major, h_major, h_major,
            _resident((N_X_TILES, 1, LANES)),
            _resident((N_X_TILES, 1, LANES)),
        ],
        out_specs=pl.BlockSpec((N_X_TILES, R, LANES), lambda b, c: (0, b * n_steps + c, 0)),
        out_shape=jax.ShapeDtypeStruct((N_X_TILES, m, LANES), BF16),
        scratch_shapes=[pltpu.VMEM((N_X_TILES, SSD_D_STATE, LANES), F32)],
        compiler_params=_params("arbitrary", "arbitrary"),
        name="ssd_core",
    )(p, c, bt, cum, cumt, dtt, wendt, dskip, norm_w)


def _mlp_residual(h, g1_ref, wu_ref, wd_ref, g2_ref):
    u = _rms(h, g1_ref[...]).astype(BF16)
    acc = None
    for c in range(0, D_FF, FF_CHUNK):
        a = jnp.dot(u, wu_ref[:, c:c + FF_CHUNK], preferred_element_type=F32)
        a = jnp.square(jnp.maximum(a, 0.0)).astype(BF16)
        part = jnp.dot(a, wd_ref[c:c + FF_CHUNK, :], preferred_element_type=F32)
        acc = part if acc is None else acc + part
    return h + _rms(acc, g2_ref[...])


def _ssd_tail_kernel(y_ref, wo_ref, h_ref, gp_ref, g1_ref, wu_ref, wd_ref, g2_ref,
                     gq_ref, wq_ref, bq_ref, o_ref, qkv_ref):
    y = jnp.concatenate([y_ref[k] for k in range(N_X_TILES)], axis=1)
    mix = jnp.dot(y, wo_ref[...], preferred_element_type=F32)
    h = _mlp_residual(h_ref[...] + _rms(mix, gp_ref[...]), g1_ref, wu_ref, wd_ref, g2_ref)
    o_ref[...] = h
    xn = _rms(h, gq_ref[...]).astype(BF16)
    qkv = jnp.dot(xn, wq_ref[...], preferred_element_type=F32) + bq_ref[...]
    qkv_ref[...] = qkv.astype(qkv_ref.dtype)


def _attn_tail_kernel(a_ref, wo_ref, bo_ref, h_ref, gp_ref, g1_ref, wu_ref, wd_ref, g2_ref, o_ref):
    mix = jnp.dot(a_ref[...], wo_ref[...], preferred_element_type=F32) + bo_ref[...]
    o_ref[...] = _mlp_residual(h_ref[...] + _rms(mix, gp_ref[...]), g1_ref, wu_ref, wd_ref, g2_ref)


_ROWS = pl.BlockSpec((ROW_TILE, D_MODEL), lambda i: (i, 0))
_GAIN = (1, D_MODEL)


def _ssd_tail(y_tiles, w_out, h, g_post, g1, wu, wd, layer, g2, g_qkv, w_qkv, b_qkv):
    m = h.shape[0]
    return pl.pallas_call(
        _ssd_tail_kernel,
        grid=(m // ROW_TILE,),
        in_specs=[
            pl.BlockSpec((N_X_TILES, ROW_TILE, LANES), lambda i: (0, i, 0)),
            _resident(w_out.shape), _ROWS, _resident(_GAIN),
            _resident(_GAIN), _resident_layer(wu.shape, layer), _resident_layer(wd.shape, layer), _resident(_GAIN),
            _resident(_GAIN), _resident(w_qkv.shape), _resident(b_qkv.shape),
        ],
        out_specs=[_ROWS, pl.BlockSpec((ROW_TILE, ATTN_QKV_DIM), lambda i: (i, 0))],
        out_shape=[jax.ShapeDtypeStruct((m, D_MODEL), F32),
                   jax.ShapeDtypeStruct((m, ATTN_QKV_DIM), BF16)],
        compiler_params=_params("arbitrary"),
        name="ssd_tail_mlp_qkv",
    )(y_tiles, w_out, h, g_post, g1, wu, wd, g2, g_qkv, w_qkv, b_qkv)


def _attn_tail(a, w_o, b_o, h, g_post, g1, wu, wd, layer, g2):
    m = h.shape[0]
    return pl.pallas_call(
        _attn_tail_kernel,
        grid=(m // ROW_TILE,),
        in_specs=[
            pl.BlockSpec((ROW_TILE, ATTN_Q_DIM), lambda i: (i, 0)),
            _resident(w_o.shape), _resident(_GAIN), _ROWS, _resident(_GAIN),
            _resident(_GAIN), _resident_layer(wu.shape, layer), _resident_layer(wd.shape, layer), _resident(_GAIN),
        ],
        out_specs=_ROWS,
        out_shape=jax.ShapeDtypeStruct((m, D_MODEL), F32),
        compiler_params=_params("arbitrary"),
        name="attn_tail_mlp",
    )(a, w_o, b_o, h, g_post, g1, wu, wd, g2)


def _attn_kernel(sink_ref, q_ref, kp_ref, kc_ref, vp_ref, vc_ref, o_ref):
    W = ATTN_WINDOW
    PACK = 16
    top = lax.broadcasted_iota(jnp.int32, (PACK, 2 * LANES), 0) == 0
    zero = jnp.zeros((), BF16)

    def window(prev_ref, cur_ref, blk):
        prev = prev_ref[...] if blk == 0 else cur_ref[(blk - 1) * W:blk * W, :]
        return jnp.concatenate(
            [jnp.where(top, zero, prev[:PACK]), prev[PACK:], cur_ref[blk * W:(blk + 1) * W, :]], axis=0)

    qrow = lax.broadcasted_iota(jnp.int32, (2 * W, 2 * W), 0)
    kpos = lax.broadcasted_iota(jnp.int32, (2 * W, 2 * W), 1)
    qpos = jnp.where(qrow >= W, qrow - W, qrow) + W
    rel = qpos - kpos
    band = (rel >= 0) & (rel < W)
    lane = lax.broadcasted_iota(jnp.int32, (W, LANES), 1)
    lane_lo = lane < HEAD_DIM
    q_extra = jnp.concatenate([jnp.where(lane < 3, 1.0, 0.0),
                               jnp.where((lane >= 3) & (lane < 6), 1.0, 0.0)], axis=0).astype(BF16)
    lane1 = lax.broadcasted_iota(jnp.int32, (1, LANES), 1)
    is_piece0 = (lane1 == 0) | (lane1 == 3)
    is_piece1 = (lane1 == 1) | (lane1 == 4)
    k_extras = []
    for t in range(ATTN_Q_DIM // LANES):
        sv = sink_ref[t]
        s_hi = sv.astype(BF16).astype(F32)
        s_mid = (sv - s_hi).astype(BF16).astype(F32)
        s_lo = (sv - s_hi - s_mid).astype(BF16).astype(F32)
        srow = jnp.where(is_piece0, s_hi, jnp.where(is_piece1, s_mid, s_lo))
        k_extras.append(jnp.concatenate(
            [jnp.where(top[:, :LANES], jnp.broadcast_to(srow, (PACK, LANES)), 0.0).astype(BF16),
             jnp.zeros((2 * W - PACK, LANES), BF16)], axis=0))

    ones_keys = jnp.ones((2 * W, LANES), BF16)
    for blk in range(ATTN_BLOCKS_PER_STEP):
        rows = slice(blk * W, (blk + 1) * W)
        kcat = window(kp_ref, kc_ref, blk)
        vcat = window(vp_ref, vc_ref, blk)
        if blk == 0:
            kmin = jnp.where(pl.program_id(1) == 0, W, 0)
            valid = (band & (kpos >= kmin)) | (kpos == 0)
        else:
            valid = band | (kpos == 0)
        bias = jnp.where(valid, 0.0, -jnp.inf)
        for t in range(ATTN_Q_DIM // LANES):
            pair = t // ATTN_REP
            qt = q_ref[rows, t * LANES:(t + 1) * LANES] * jnp.asarray(HEAD_DIM ** -0.5, BF16)
            qs = jnp.concatenate([jnp.where(lane_lo, qt, zero), jnp.where(lane_lo, zero, qt)], axis=0)
            kt = jnp.concatenate([kcat[:, pair * LANES:(pair + 1) * LANES], k_extras[t]], axis=1)
            vt = jnp.concatenate([vcat[:, pair * LANES:(pair + 1) * LANES], ones_keys], axis=1)
            s = lax.dot_general(jnp.concatenate([qs, q_extra], axis=1), kt, (((1,), (1,)), ((), ())),
                                preferred_element_type=F32) + bias
            mx = jnp.max(s, axis=-1, keepdims=True)
            e = jnp.exp((s - mx).astype(BF16))
            pv = jnp.dot(e, vt, preferred_element_type=F32)
            pv = pv[:, :LANES] * (1.0 / pv[:, LANES:])
            o_ref[rows, t * LANES:(t + 1) * LANES] = jnp.where(lane_lo, pv[:W], pv[W:]).astype(o_ref.dtype)


def _attention(qkv, sinks, batch):
    m = qkv.shape[0]
    W = ATTN_WINDOW
    R = ATTN_BLOCKS_PER_STEP * W
    nb = m // batch // W
    n_steps = nb // ATTN_BLOCKS_PER_STEP
    k_col = ATTN_Q_DIM // ATTN_KV_DIM
    v_col = k_col + 1

    def cur(b, i):
        return b * n_steps + i

    def prev(b, i):
        return jnp.maximum(b * nb + ATTN_BLOCKS_PER_STEP * i - 1, 0)

    return pl.pallas_call(
        _attn_kernel,
        grid=(batch, n_steps),
        in_specs=[
            _resident(sinks.shape),
            pl.BlockSpec((R, ATTN_Q_DIM), lambda b, i: (cur(b, i), 0)),
            pl.BlockSpec((W, ATTN_KV_DIM), lambda b, i: (prev(b, i), k_col)),
            pl.BlockSpec((R, ATTN_KV_DIM), lambda b, i: (cur(b, i), k_col)),
            pl.BlockSpec((W, ATTN_KV_DIM), lambda b, i: (prev(b, i), v_col)),
            pl.BlockSpec((R, ATTN_KV_DIM), lambda b, i: (cur(b, i), v_col)),
        ],
        out_specs=pl.BlockSpec((R, ATTN_Q_DIM), lambda b, i: (cur(b, i), 0)),
        out_shape=jax.ShapeDtypeStruct((m, ATTN_Q_DIM), BF16),
        compiler_params=_params("arbitrary", "arbitrary"),
        name="swa_sink_attention",
    )(sinks, qkv, qkv, qkv, qkv, qkv)


def _permute_q_heads(v, axis):
    inner = v.shape[axis] // ATTN_N_Q_HEADS
    shape = v.shape[:axis] + (ATTN_N_KV_HEADS // 2, 2, ATTN_REP, inner) + v.shape[axis + 1:]
    order = list(range(len(shape)))
    order[axis + 1], order[axis + 2] = axis + 2, axis + 1
    return v.reshape(shape).transpose(order).reshape(v.shape)


def _lane_tiles(v, n_tiles):
    return v.astype(F32).reshape(n_tiles, 1, LANES)


def kernel(x, ssd_w_in, ssd_conv_w, ssd_conv_b, ssd_dt_bias, ssd_a_log, ssd_d, ssd_norm_w, ssd_w_out, attn_w_qkv, attn_b_qkv, attn_sinks, attn_w_o, attn_b_o, mlp_w_up, mlp_w_down, mix_pre_norm, mix_post_norm, ffn_pre_norm, ffn_post_norm):
    batch, seq, d = x.shape
    m = batch * seq
    h = x.reshape(m, d)
    row = lambda v: v.astype(F32).reshape(1, -1)

    w_in = ssd_w_in[0].astype(BF16)
    w_dt = jnp.pad(ssd_w_in[0][:, SSD_IN_DIM - SSD_N_HEADS:], ((0, 0), (0, LANES - SSD_N_HEADS))).astype(BF16)
    conv_w = ssd_conv_w[0].astype(F32).reshape(SSD_CONV_WIDTH, N_CONV_TILES, LANES).transpose(1, 0, 2)
    pad_heads = lambda v: jnp.pad(v.astype(F32), (0, LANES - SSD_N_HEADS)).reshape(1, LANES)
    assert seq == SEQ and seq % ROW_TILE == 0
    p, c, bt, cum, cumt, dtt, wendt = _ssd_in_proj(
        h, row(mix_pre_norm[0]), w_in, w_dt, conv_w, _lane_tiles(ssd_conv_b[0], N_CONV_TILES),
        pad_heads(ssd_dt_bias[0]), pad_heads(ssd_a_log[0]))
    y = _ssd_core(
        p, c, bt, cum, cumt, dtt, wendt, _lane_tiles(jnp.repeat(ssd_d[0], HEAD_DIM), N_X_TILES),
        _lane_tiles(ssd_norm_w[0], N_X_TILES), batch)
    wq, wkv = attn_w_qkv[0][:, :ATTN_Q_DIM], attn_w_qkv[0][:, ATTN_Q_DIM:]
    bq, bkv = attn_b_qkv[0][:ATTN_Q_DIM], attn_b_qkv[0][ATTN_Q_DIM:]
    w_qkv = jnp.concatenate([_permute_q_heads(wq, 1), wkv], axis=1).astype(BF16)
    b_qkv = jnp.concatenate([_permute_q_heads(bq, 0), bkv]).astype(F32).reshape(1, -1)
    w_up, w_down = mlp_w_up.astype(BF16), mlp_w_down.astype(BF16)
    h, qkv = _ssd_tail(
        y, ssd_w_out[0].astype(BF16), h, row(mix_post_norm[0]),
        row(ffn_pre_norm[0]), w_up, w_down, 0, row(ffn_post_norm[0]),
        row(mix_pre_norm[1]), w_qkv, b_qkv)

    sink_pairs = _permute_q_heads(attn_sinks[0], 0).astype(F32).reshape(ATTN_Q_DIM // LANES, 2, 1)
    sink_rows = jnp.pad(jnp.broadcast_to(sink_pairs, sink_pairs.shape[:2] + (3,)).reshape(-1, 1, 6),
                        ((0, 0), (0, 0), (0, LANES - 6)))
    a = _attention(qkv, sink_rows, batch)
    h = _attn_tail(
        a, _permute_q_heads(attn_w_o[0], 0).astype(BF16), row(attn_b_o[0]), h, row(mix_post_norm[1]),
        row(ffn_pre_norm[1]), w_up, w_down, 1, row(ffn_post_norm[1]))
    return h.reshape(batch, seq, d)
```

```python
import functools

import jax
import jax.numpy as jnp
from jax import lax
from jax.experimental import pallas as pl
from jax.experimental.pallas import tpu as pltpu

F32 = jnp.float32
BF16 = jnp.bfloat16

D_MODEL = 1024
SEQ = 4096
NORM_EPS = 1e-6
LOG2_E = 1.4426950408889634

LANES = 128
HEAD_DIM = 64

SSD_D_INNER = 2048
SSD_N_HEADS = 32
SSD_N_GROUPS = 8
SSD_D_STATE = 128
SSD_CONV_WIDTH = 4
SSD_CHUNK = 128
SSD_CONV_DIM = SSD_D_INNER + 2 * SSD_N_GROUPS * SSD_D_STATE
SSD_IN_DIM = SSD_D_INNER + SSD_CONV_DIM + SSD_N_HEADS
N_Z_TILES = SSD_D_INNER // LANES
N_X_TILES = SSD_D_INNER // LANES
N_CONV_TILES = SSD_CONV_DIM // LANES
N_IN_TILES = N_Z_TILES + N_CONV_TILES + 1
B_TILE0 = N_X_TILES
C_TILE0 = N_X_TILES + SSD_N_GROUPS
HALO_ROWS = 8
SSD_CHUNKS_PER_STEP = 4
N_STAGE = 4
P_X0 = N_Z_TILES
N_P_TILES = N_Z_TILES + N_X_TILES

ATTN_N_Q_HEADS = 16
ATTN_N_KV_HEADS = 4
ATTN_REP = 4
ATTN_WINDOW = 128
ATTN_BLOCKS_PER_STEP = 16
ATTN_Q_DIM = ATTN_N_Q_HEADS * HEAD_DIM
ATTN_KV_DIM = ATTN_N_KV_HEADS * HEAD_DIM
ATTN_QKV_DIM = ATTN_Q_DIM + 2 * ATTN_KV_DIM

D_FF = 4096
FF_CHUNK = 1024

ROW_TILE = 512
VMEM_LIMIT = 56 * 1024 * 1024


def _rms(x, gain):
    ms = jnp.mean(x * x, axis=-1, keepdims=True)
    return x * lax.rsqrt(ms + NORM_EPS) * gain


def _sigmoid(x):
    return 1.0 / (1.0 + jnp.exp2(x * (-LOG2_E)))


def _resident(shape):
    nd = len(shape)
    return pl.BlockSpec(shape, lambda *_: (0,) * nd, pipeline_mode=pl.Buffered(1))


def _resident_layer(stacked_shape, layer):
    return pl.BlockSpec((None,) + tuple(stacked_shape[1:]), lambda *_: (layer, 0, 0),
                        pipeline_mode=pl.Buffered(1))


def _params(*sem):
    return pltpu.CompilerParams(dimension_semantics=sem, vmem_limit_bytes=VMEM_LIMIT)


def _ssd_in_proj_kernel(x_ref, g_ref, w_ref, wdt_ref, cw_ref, cb_ref, dtb_ref, alog_ref,
                        p_ref, c_ref, bt_ref, cum_ref, cumt_ref, dtt_ref, wendt_ref, halo_ref, stage_ref):
    R = ROW_TILE
    T = SSD_CHUNK

    @pl.when(pl.program_id(0) % (SEQ // ROW_TILE) == 0)
    def _():
        halo_ref[...] = jnp.zeros_like(halo_ref)

    xn = _rms(x_ref[...], g_ref[...]).astype(BF16)
    pair = 2 * LANES

    dt_in = jnp.dot(xn, wdt_ref[...], preferred_element_type=F32) + dtb_ref[...]
    dt = jnp.maximum(dt_in, 0.0) + jnp.log1p(jnp.exp(-jnp.abs(dt_in)))
    aq = dt * (-jnp.exp(alog_ref[...]))
    tri = (lax.broadcasted_iota(jnp.int32, (T, T), 0)
           >= lax.broadcasted_iota(jnp.int32, (T, T), 1)).astype(BF16)
    aq_hi = aq.astype(BF16)
    rem = aq - aq_hi.astype(F32)
    aq_mid = rem.astype(BF16)
    aq_lo = (rem - aq_mid.astype(F32)).astype(BF16)
    for c in range(R // T):
        rows = slice(c * T, (c + 1) * T)
        cum = (jnp.dot(tri, aq_hi[rows], preferred_element_type=F32)
               + jnp.dot(tri, aq_mid[rows], preferred_element_type=F32)
               + jnp.dot(tri, aq_lo[rows], preferred_element_type=F32))
        cum_last = cum[T - 1:T, :]
        cum2 = cum * LOG2_E
        cum_ref[rows, :] = cum2
        cumt_ref[:, rows] = cum2.T
        dtt_ref[:, rows] = dt[rows].T
        wendt_ref[:, rows] = (dt[rows] * jnp.exp(cum_last - cum)).T

    def z_pair(t):
        acc = jnp.dot(xn, w_ref[:, t * LANES:t * LANES + pair], preferred_element_type=F32)
        gate = acc * _sigmoid(acc)
        p_ref[t] = gate[:, :LANES]
        p_ref[t + 1] = gate[:, LANES:]

    def conv_silu(raw, kc):
        w = cw_ref[kc]
        slot = kc % N_STAGE
        stage_ref[slot, 0:HALO_ROWS, :] = halo_ref[kc]
        stage_ref[slot, HALO_ROWS:, :] = raw
        acc = cb_ref[kc] + w[SSD_CONV_WIDTH - 1:SSD_CONV_WIDTH] * raw
        for d in range(1, SSD_CONV_WIDTH):
            shifted = stage_ref[slot, HALO_ROWS - d:HALO_ROWS - d + R, :]
            acc = acc + w[SSD_CONV_WIDTH - 1 - d:SSD_CONV_WIDTH - d] * shifted
        halo_ref[kc] = raw[R - HALO_ROWS:]
        return acc * _sigmoid(acc)

    def put_conv(kc, val):
        if kc < B_TILE0:
            p_ref[P_X0 + kc] = val
        elif kc < C_TILE0:
            for c in range(R // T):
                bt_ref[kc - B_TILE0, :, c * T:(c + 1) * T] = val[c * T:(c + 1) * T].T.astype(bt_ref.dtype)
        else:
            c_ref[kc - C_TILE0] = val.astype(c_ref.dtype)

    def conv_pair(t):
        col = (N_Z_TILES + t) * LANES
        acc = jnp.dot(xn, w_ref[:, col:col + pair], preferred_element_type=F32)
        put_conv(t, conv_silu(acc[:, :LANES], t))
        put_conv(t + 1, conv_silu(acc[:, LANES:], t + 1))

    for i in range(N_Z_TILES // 2):
        conv_pair(4 * i)
        z_pair(2 * i)
        conv_pair(4 * i + 2)


def _ssd_in_proj(x2d, gain, w_in, w_dt, conv_w, conv_b, dt_bias, a_log):
    m = x2d.shape[0]
    t_major = pl.BlockSpec((ROW_TILE, LANES), lambda i: (i, 0))
    h_major = pl.BlockSpec((LANES, ROW_TILE), lambda i: (0, i))
    return pl.pallas_call(
        _ssd_in_proj_kernel,
        grid=(m // ROW_TILE,),
        in_specs=[
            pl.BlockSpec((ROW_TILE, D_MODEL), lambda i: (i, 0)),
            _resident((1, D_MODEL)),
            _resident(w_in.shape),
            _resident(w_dt.shape),
            _resident((N_CONV_TILES, SSD_CONV_WIDTH, LANES)),
            _resident((N_CONV_TILES, 1, LANES)),
            _resident((1, LANES)),
            _resident((1, LANES)),
        ],
        out_specs=[
            pl.BlockSpec((N_P_TILES, ROW_TILE, LANES), lambda i: (0, i, 0)),
            pl.BlockSpec((SSD_N_GROUPS, ROW_TILE, LANES), lambda i: (0, i, 0)),
            pl.BlockSpec((SSD_N_GROUPS, SSD_D_STATE, ROW_TILE), lambda i: (0, 0, i)),
            t_major, h_major, h_major, h_major,
        ],
        out_shape=[
            jax.ShapeDtypeStruct((N_P_TILES, m, LANES), F32),
            jax.ShapeDtypeStruct((SSD_N_GROUPS, m, LANES), F32),
            jax.ShapeDtypeStruct((SSD_N_GROUPS, SSD_D_STATE, m), F32),
            jax.ShapeDtypeStruct((m, LANES), F32),
            jax.ShapeDtypeStruct((LANES, m), F32),
            jax.ShapeDtypeStruct((LANES, m), F32),
            jax.ShapeDtypeStruct((LANES, m), F32),
        ],
        scratch_shapes=[
            pltpu.VMEM((N_CONV_TILES, HALO_ROWS, LANES), F32),
            pltpu.VMEM((N_STAGE, HALO_ROWS + ROW_TILE, LANES), F32),
        ],
        compiler_params=_params("arbitrary"),
        name="ssd_in_proj",
    )(x2d, gain, w_in, w_dt, conv_w, conv_b, dt_bias, a_log)


def _ssd_kernel(p_ref, c_ref, bt_ref, cum_ref, cumt_ref, dtt_ref, wendt_ref, dskip_ref, nw_ref, o_ref, state_ref):
    T = SSD_CHUNK
    lane = lax.broadcasted_iota(jnp.int32, (T, LANES), 1)
    row = lax.broadcasted_iota(jnp.int32, (T, LANES), 0)
    lane_lo = lane < HEAD_DIM
    causal = row >= lane

    @pl.when(pl.program_id(1) == 0)
    def _():
        state_ref[...] = jnp.zeros_like(state_ref)

    def group(rows, dec, g):
        cm = c_ref[g, rows, :].astype(BF16)
        bt = bt_ref[g, :, rows].astype(BF16)
        cb = jnp.dot(cm, bt, preferred_element_type=F32).astype(BF16)
        gated = []
        for half in range(2):
            kx = 2 * g + half
            x = p_ref[P_X0 + kx, rows, :]
            st = state_ref[kx]
            xb = x.astype(BF16)
            lhs_y = []
            lhs_s = []
            for i in range(2):
                h = 2 * kx + i
                crow = cumt_ref[h:h + 1, rows]
                cbc = jnp.broadcast_to(cum_ref[rows, h:h + 1], (T, LANES))
                decay = jnp.where(causal, jnp.exp2(cbc - crow), 0.0).astype(BF16)
                lhs_y.append(jnp.concatenate(
                    [cb * decay * dtt_ref[h:h + 1, rows].astype(BF16), cm * jnp.exp2(cbc).astype(BF16)],
                    axis=1))
                lhs_s.append(bt * wendt_ref[h:h + 1, rows].astype(BF16))
            yy = jnp.dot(jnp.concatenate(lhs_y, axis=0), jnp.concatenate([xb, st.astype(BF16)], axis=0),
                         preferred_element_type=F32)
            y = jnp.where(lane_lo, yy[:T], yy[T:]) + dskip_ref[kx] * x
            ds = jnp.dot(jnp.concatenate(lhs_s, axis=0), xb, preferred_element_type=F32)
            d0 = jnp.broadcast_to(dec[:, 2 * kx:2 * kx + 1], (1, LANES))
            d1 = jnp.broadcast_to(dec[:, 2 * kx + 1:2 * kx + 2], (1, LANES))
            state_ref[kx] = st * jnp.where(lane_lo[0:1], d0, d1) + jnp.where(
                lane_lo, ds[:SSD_D_STATE], ds[SSD_D_STATE:])
            gated.append(y * p_ref[kx, rows, :])
        ss = (jnp.sum(gated[0] * gated[0], axis=-1, keepdims=True)
              + jnp.sum(gated[1] * gated[1], axis=-1, keepdims=True))
        r = lax.rsqrt(ss * (1.0 / (2 * LANES)) + NORM_EPS)
        for half in range(2):
            kx = 2 * g + half
            o_ref[kx, rows, :] = (gated[half] * r * nw_ref[kx]).astype(o_ref.dtype)

    decs = [jnp.exp2(cum_ref[(ci + 1) * T - 1:(ci + 1) * T, :])
            for ci in range(SSD_CHUNKS_PER_STEP)]
    for g in range(SSD_N_GROUPS):
        for ci in range(SSD_CHUNKS_PER_STEP):
            group(slice(ci * T, (ci + 1) * T), decs[ci], g)


def _ssd_core(p, c, bt, cum, cumt, dtt, wendt, dskip, norm_w, batch):
    m = p.shape[1]
    R = SSD_CHUNKS_PER_STEP * SSD_CHUNK
    n_steps = m // batch // R
    h_major = pl.BlockSpec((LANES, R), lambda b, c: (0, b * n_steps + c))
    return pl.pallas_call(
        _ssd_kernel,
        grid=(batch, n_steps),
        in_specs=[
            pl.BlockSpec((N_P_TILES, R, LANES), lambda b, c: (0, b * n_steps + c, 0)),
            pl.BlockSpec((SSD_N_GROUPS, R, LANES), lambda b, c: (0, b * n_steps + c, 0)),
            pl.BlockSpec((SSD_N_GROUPS, SSD_D_STATE, R), lambda b, c: (0, 0, b * n_steps + c)),
            pl.BlockSpec((R, LANES), lambda b, c: (b * n_steps + c, 0)),
            h_major, h_major, h_major,
            _resident((N_X_TILES, 1, LANES)),
            _resident((N_X_TILES, 1, LANES)),
        ],
        out_specs=pl.BlockSpec((N_X_TILES, R, LANES), lambda b, c: (0, b * n_steps + c, 0)),
        out_shape=jax.ShapeDtypeStruct((N_X_TILES, m, LANES), BF16),
        scratch_shapes=[pltpu.VMEM((N_X_TILES, SSD_D_STATE, LANES), F32)],
        compiler_params=_params("arbitrary", "arbitrary"),
        name="ssd_core",
    )(p, c, bt, cum, cumt, dtt, wendt, dskip, norm_w)


def _mlp_residual(h, g1_ref, wu_ref, wd_ref, g2_ref):
    u = _rms(h, g1_ref[...]).astype(BF16)
    acc = None
    for c in range(0, D_FF, FF_CHUNK):
        a = jnp.dot(u, wu_ref[:, c:c + FF_CHUNK], preferred_element_type=F32)
        a = jnp.square(jnp.maximum(a, 0.0)).astype(BF16)
        part = jnp.dot(a, wd_ref[c:c + FF_CHUNK, :], preferred_element_type=F32)
        acc = part if acc is None else acc + part
    return h + _rms(acc, g2_ref[...])


def _ssd_tail_kernel(y_ref, wo_ref, h_ref, gp_ref, g1_ref, wu_ref, wd_ref, g2_ref,
                     gq_ref, wq_ref, bq_ref, o_ref, qkv_ref):
    y = jnp.concatenate([y_ref[k] for k in range(N_X_TILES)], axis=1)
    mix = jnp.dot(y, wo_ref[...], preferred_element_type=F32)
    h = _mlp_residual(h_ref[...] + _rms(mix, gp_ref[...]), g1_ref, wu_ref, wd_ref, g2_ref)
    o_ref[...] = h
    xn = _rms(h, gq_ref[...]).astype(BF16)
    qkv = jnp.dot(xn, wq_ref[...], preferred_element_type=F32) + bq_ref[...]
    qkv_ref[...] = qkv.astype(qkv_ref.dtype)


def _attn_tail_kernel(a_ref, wo_ref, bo_ref, h_ref, gp_ref, g1_ref, wu_ref, wd_ref, g2_ref, o_ref):
    mix = jnp.dot(a_ref[...], wo_ref[...], preferred_element_type=F32) + bo_ref[...]
    o_ref[...] = _mlp_residual(h_ref[...] + _rms(mix, gp_ref[...]), g1_ref, wu_ref, wd_ref, g2_ref)


_ROWS = pl.BlockSpec((ROW_TILE, D_MODEL), lambda i: (i, 0))
_GAIN = (1, D_MODEL)


def _ssd_tail(y_tiles, w_out, h, g_post, g1, wu, wd, layer, g2, g_qkv, w_qkv, b_qkv):
    m = h.shape[0]
    return pl.pallas_call(
        _ssd_tail_kernel,
        grid=(m // ROW_TILE,),
        in_specs=[
            pl.BlockSpec((N_X_TILES, ROW_TILE, LANES), lambda i: (0, i, 0)),
            _resident(w_out.shape), _ROWS, _resident(_GAIN),
            _resident(_GAIN), _resident_layer(wu.shape, layer), _resident_layer(wd.shape, layer), _resident(_GAIN),
            _resident(_GAIN), _resident(w_qkv.shape), _resident(b_qkv.shape),
        ],
        out_specs=[_ROWS, pl.BlockSpec((ROW_TILE, ATTN_QKV_DIM), lambda i: (i, 0))],
        out_shape=[jax.ShapeDtypeStruct((m, D_MODEL), F32),
                   jax.ShapeDtypeStruct((m, ATTN_QKV_DIM), BF16)],
        compiler_params=_params("arbitrary"),
        name="ssd_tail_mlp_qkv",
    )(y_tiles, w_out, h, g_post, g1, wu, wd, g2, g_qkv, w_qkv, b_qkv)


def _attn_tail(a, w_o, b_o, h, g_post, g1, wu, wd, layer, g2):
    m = h.shape[0]
    return pl.pallas_call(
        _attn_tail_kernel,
        grid=(m // ROW_TILE,),
        in_specs=[
            pl.BlockSpec((ROW_TILE, ATTN_Q_DIM), lambda i: (i, 0)),
            _resident(w_o.shape), _resident(_GAIN), _ROWS, _resident(_GAIN),
            _resident(_GAIN), _resident_layer(wu.shape, layer), _resident_layer(wd.shape, layer), _resident(_GAIN),
        ],
        out_specs=_ROWS,
        out_shape=jax.ShapeDtypeStruct((m, D_MODEL), F32),
        compiler_params=_params("arbitrary"),
        name="attn_tail_mlp",
    )(a, w_o, b_o, h, g_post, g1, wu, wd, g2)


def _attn_kernel(sink_ref, q_ref, kp_ref, kc_ref, vp_ref, vc_ref, o_ref):
    W = ATTN_WINDOW
    PACK = 16
    top = lax.broadcasted_iota(jnp.int32, (PACK, 2 * LANES), 0) == 0
    zero = jnp.zeros((), BF16)

    def window(prev_ref, cur_ref, blk):
        prev = prev_ref[...] if blk == 0 else cur_ref[(blk - 1) * W:blk * W, :]
        return jnp.concatenate(
            [jnp.where(top, zero, prev[:PACK]), prev[PACK:], cur_ref[blk * W:(blk + 1) * W, :]], axis=0)

    qrow = lax.broadcasted_iota(jnp.int32, (2 * W, 2 * W), 0)
    kpos = lax.broadcasted_iota(jnp.int32, (2 * W, 2 * W), 1)
    qpos = jnp.where(qrow >= W, qrow - W, qrow) + W
    rel = qpos - kpos
    band = (rel >= 0) & (rel < W)
    lane = lax.broadcasted_iota(jnp.int32, (W, LANES), 1)
    lane_lo = lane < HEAD_DIM
    q_extra = jnp.concatenate([jnp.where(lane < 3, 1.0, 0.0),
                               jnp.where((lane >= 3) & (lane < 6), 1.0, 0.0)], axis=0).astype(BF16)
    lane1 = lax.broadcasted_iota(jnp.int32, (1, LANES), 1)
    is_piece0 = (lane1 == 0) | (lane1 == 3)
    is_piece1 = (lane1 == 1) | (lane1 == 4)
    k_extras = []
    for t in range(ATTN_Q_DIM // LANES):
        sv = sink_ref[t]
        s_hi = sv.astype(BF16).astype(F32)
        s_mid = (sv - s_hi).astype(BF16).astype(F32)
        s_lo = (sv - s_hi - s_mid).astype(BF16).astype(F32)
        srow = jnp.where(is_piece0, s_hi, jnp.where(is_piece1, s_mid, s_lo))
        k_extras.append(jnp.concatenate(
            [jnp.where(top[:, :LANES], jnp.broadcast_to(srow, (PACK, LANES)), 0.0).astype(BF16),
             jnp.zeros((2 * W - PACK, LANES), BF16)], axis=0))

    ones_keys = jnp.ones((2 * W, LANES), BF16)
    for blk in range(ATTN_BLOCKS_PER_STEP):
        rows = slice(blk * W, (blk + 1) * W)
        kcat = window(kp_ref, kc_ref, blk)
        vcat = window(vp_ref, vc_ref, blk)
        if blk == 0:
            kmin = jnp.where(pl.program_id(1) == 0, W, 0)
            valid = (band & (kpos >= kmin)) | (kpos == 0)
        else:
            valid = band | (kpos == 0)
        bias = jnp.where(valid, 0.0, -jnp.inf)
        for t in range(ATTN_Q_DIM // LANES):
            pair = t // ATTN_REP
            qt = q_ref[rows, t * LANES:(t + 1) * LANES] * jnp.asarray(HEAD_DIM ** -0.5, BF16)
            qs = jnp.concatenate([jnp.where(lane_lo, qt, zero), jnp.where(lane_lo, zero, qt)], axis=0)
            kt = jnp.concatenate([kcat[:, pair * LANES:(pair + 1) * LANES], k_extras[t]], axis=1)
            vt = jnp.concatenate([vcat[:, pair * LANES:(pair + 1) * LANES], ones_keys], axis=1)
            s = lax.dot_general(jnp.concatenate([qs, q_extra], axis=1), kt, (((1,), (1,)), ((), ())),
                                preferred_element_type=F32) + bias
            mx = jnp.max(s, axis=-1, keepdims=True)
            e = jnp.exp((s - mx).astype(BF16))
            pv = jnp.dot(e, vt, preferred_element_type=F32)
            pv = pv[:, :LANES] * (1.0 / pv[:, LANES:])
            o_ref[rows, t * LANES:(t + 1) * LANES] = jnp.where(lane_lo, pv[:W], pv[W:]).astype(o_ref.dtype)


def _attention(qkv, sinks, batch):
    m = qkv.shape[0]
    W = ATTN_WINDOW
    R = ATTN_BLOCKS_PER_STEP * W
    nb = m // batch // W
    n_steps = nb // ATTN_BLOCKS_PER_STEP
    k_col = ATTN_Q_DIM // ATTN_KV_DIM
    v_col = k_col + 1

    def cur(b, i):
        return b * n_steps + i

    def prev(b, i):
        return jnp.maximum(b * nb + ATTN_BLOCKS_PER_STEP * i - 1, 0)

    return pl.pallas_call(
        _attn_kernel,
        grid=(batch, n_steps),
        in_specs=[
            _resident(sinks.shape),
            pl.BlockSpec((R, ATTN_Q_DIM), lambda b, i: (cur(b, i), 0)),
            pl.BlockSpec((W, ATTN_KV_DIM), lambda b, i: (prev(b, i), k_col)),
            pl.BlockSpec((R, ATTN_KV_DIM), lambda b, i: (cur(b, i), k_col)),
            pl.BlockSpec((W, ATTN_KV_DIM), lambda b, i: (prev(b, i), v_col)),
            pl.BlockSpec((R, ATTN_KV_DIM), lambda b, i: (cur(b, i), v_col)),
        ],
        out_specs=pl.BlockSpec((R, ATTN_Q_DIM), lambda b, i: (cur(b, i), 0)),
        out_shape=jax.ShapeDtypeStruct((m, ATTN_Q_DIM), BF16),
        compiler_params=_params("arbitrary", "arbitrary"),
        name="swa_sink_attention",
    )(sinks, qkv, qkv, qkv, qkv, qkv)


def _permute_q_heads(v, axis):
    inner = v.shape[axis] // ATTN_N_Q_HEADS
    shape = v.shape[:axis] + (ATTN_N_KV_HEADS // 2, 2, ATTN_REP, inner) + v.shape[axis + 1:]
    order = list(range(len(shape)))
    order[axis + 1], order[axis + 2] = axis + 2, axis + 1
    return v.reshape(shape).transpose(order).reshape(v.shape)


def _lane_tiles(v, n_tiles):
    return v.astype(F32).reshape(n_tiles, 1, LANES)


def kernel(x, ssd_w_in, ssd_conv_w, ssd_conv_b, ssd_dt_bias, ssd_a_log, ssd_d, ssd_norm_w, ssd_w_out, attn_w_qkv, attn_b_qkv, attn_sinks, attn_w_o, attn_b_o, mlp_w_up, mlp_w_down, mix_pre_norm, mix_post_norm, ffn_pre_norm, ffn_post_norm):
    batch, seq, d = x.shape
    m = batch * seq
    h = x.reshape(m, d)
    row = lambda v: v.astype(F32).reshape(1, -1)

    w_in = ssd_w_in[0].astype(BF16)
    w_dt = jnp.pad(ssd_w_in[0][:, SSD_IN_DIM - SSD_N_HEADS:], ((0, 0), (0, LANES - SSD_N_HEADS))).astype(BF16)
    conv_w = ssd_conv_w[0].astype(F32).reshape(SSD_CONV_WIDTH, N_CONV_TILES, LANES).transpose(1, 0, 2)
    pad_heads = lambda v: jnp.pad(v.astype(F32), (0, LANES - SSD_N_HEADS)).reshape(1, LANES)
    assert seq == SEQ and seq % ROW_TILE == 0
    p, c, bt, cum, cumt, dtt, wendt = _ssd_in_proj(
        h, row(mix_pre_norm[0]), w_in, w_dt, conv_w, _lane_tiles(ssd_conv_b[0], N_CONV_TILES),
        pad_heads(ssd_dt_bias[0]), pad_heads(ssd_a_log[0]))
    y = _ssd_core(
        p, c, bt, cum, cumt, dtt, wendt, _lane_tiles(jnp.repeat(ssd_d[0], HEAD_DIM), N_X_TILES),
        _lane_tiles(ssd_norm_w[0], N_X_TILES), batch)
    wq, wkv = attn_w_qkv[0][:, :ATTN_Q_DIM], attn_w_qkv[0][:, ATTN_Q_DIM:]
    bq, bkv = attn_b_qkv[0][:ATTN_Q_DIM], attn_b_qkv[0][ATTN_Q_DIM:]
    w_qkv = jnp.concatenate([_permute_q_heads(wq, 1), wkv], axis=1).astype(BF16)
    b_qkv = jnp.concatenate([_permute_q_heads(bq, 0), bkv]).astype(F32).reshape(1, -1)
    w_up, w_down = mlp_w_up.astype(BF16), mlp_w_down.astype(BF16)
    h, qkv = _ssd_tail(
        y, ssd_w_out[0].astype(BF16), h, row(mix_post_norm[0]),
        row(ffn_pre_norm[0]), w_up, w_down, 0, row(ffn_post_norm[0]),
        row(mix_pre_norm[1]), w_qkv, b_qkv)

    sink_pairs = _permute_q_heads(attn_sinks[0], 0).astype(F32).reshape(ATTN_Q_DIM // LANES, 2, 1)
    sink_rows = jnp.pad(jnp.broadcast_to(sink_pairs, sink_pairs.shape[:2] + (3,)).reshape(-1, 1, 6),
                        ((0, 0), (0, 0), (0, LANES - 6)))
    a = _attention(qkv, sink_rows, batch)
    h = _attn_tail(
        a, _permute_q_heads(attn_w_o[0], 0).astype(BF16), row(attn_b_o[0]), h, row(mix_post_norm[1]),
        row(ffn_pre_norm[1]), w_up, w_down, 1, row(ffn_post_norm[1]))
    return h.reshape(batch, seq, d)
```
